```python
import jax, jax.numpy as jnp
from jax import lax
import numpy as np


D_MODEL = 1024
BATCH = 4
SEQ = 4096
DEPTH = 1
DEC_BATCH = 128
DEC_SEQ = 8
PAST_LEN = 2048
PAGE_SIZE = 128

D_POOL = D_MODEL // 2
POOL_WINDOWS = (2, 4, 8, 16)
POOL_GROUPS = len(POOL_WINDOWS)
POOL_GC = D_POOL // POOL_GROUPS
POOL_HIST = max(POOL_WINDOWS) - 1
D_ATTN = D_MODEL // 2
HEAD_DIM = 64
N_HEADS = D_ATTN // HEAD_DIM
Q_BLOCK = 128
ATTN_SCALE = HEAD_DIM ** -0.5
N_KEYS = 128
N_EXPERTS = N_KEYS * N_KEYS
PEER_HEADS = 8
PEER_TOPK = 16
D_KEY = 256
D_KEY_HALF = D_KEY // 2
PEER_CHUNK = 128
RMS_EPS = 1e-6
D_IN = D_POOL + 3 * D_ATTN + N_HEADS + 2 * D_MODEL
SPLITS = (D_POOL, D_POOL + D_ATTN, D_POOL + 2 * D_ATTN, D_POOL + 3 * D_ATTN,
          D_POOL + 3 * D_ATTN + N_HEADS, D_POOL + 3 * D_ATTN + N_HEADS + D_MODEL)

kernel_name = 'hybrid_pool_fox_peer_decode_step'


def rmsnorm(x, g):
    xf = x.astype(jnp.float32)
    ms = jnp.mean(xf * xf, axis=-1, keepdims=True)
    return (xf * lax.rsqrt(ms + RMS_EPS) * g.astype(jnp.float32)).astype(x.dtype)


def project_in(x, norm_mix_g, w_in, b_forget, q_norm_g, k_norm_g):
    B, L, _ = x.shape
    xn = rmsnorm(x, norm_mix_g)
    p = xn @ w_in
    u_pool, q, k, v, f_logit, gate_a, gate_b = jnp.split(p, SPLITS, axis=-1)
    q = rmsnorm(q.reshape(B, L, N_HEADS, HEAD_DIM), q_norm_g)
    k = rmsnorm(k.reshape(B, L, N_HEADS, HEAD_DIM), k_norm_g)
    v = v.reshape(B, L, N_HEADS, HEAD_DIM)
    logf = jax.nn.log_sigmoid((f_logit + b_forget).astype(jnp.float32))
    return u_pool, q, k, v, logf, gate_a, gate_b


def pool_mixer(u, hist, pos0, pool_mix_w, pool_scale):
    B, L, _ = u.shape
    u_ext = jnp.concatenate([hist.astype(u.dtype), u], axis=1)
    c = jnp.cumsum(u_ext.astype(jnp.float32), axis=1)
    c = jnp.concatenate([jnp.zeros((B, 1, D_POOL), jnp.float32), c], axis=1)
    pos = pos0 + jnp.arange(L)
    start = POOL_HIST + 1
    outs = []
    for gi, w in enumerate(POOL_WINDOWS):
        sl = slice(gi * POOL_GC, (gi + 1) * POOL_GC)
        wsum = c[:, start:start + L, sl] - c[:, start - w:start - w + L, sl]
        cnt = jnp.minimum(pos + 1, w).astype(jnp.float32)
        outs.append(wsum / cnt[None, :, None] - u[..., sl].astype(jnp.float32))
    m = jnp.stack(outs, axis=2)
    m = jnp.einsum('blgc,gcd->blgd', m, pool_mix_w.astype(jnp.float32)).reshape(B, L, D_POOL)
    out = (m * pool_scale).astype(u.dtype)
    return out, u_ext[:, -POOL_HIST:]


def attend(q, k, v, cq, ck, qpos, kpos):
    logits = jnp.einsum('bqhd,bkhd->bhqk', q.astype(jnp.float32), k.astype(jnp.float32)) * ATTN_SCALE
    bias = jnp.transpose(cq, (0, 2, 1))[..., :, None] - jnp.transpose(ck, (0, 2, 1))[..., None, :]
    mask = kpos[None, :] <= qpos[:, None]
    logits = jnp.where(mask, logits + bias, -jnp.inf)
    p = jax.nn.softmax(logits, axis=-1)
    o = jnp.einsum('bhqk,bkhd->bqhd', p, v.astype(jnp.float32))
    return o.astype(v.dtype)


def prompt_attention(q, k, v, logf):
    B, S = q.shape[:2]
    c = jnp.cumsum(logf.astype(jnp.float32), axis=1)
    kpos = jnp.arange(S)

    def block(i):
        s0 = i * Q_BLOCK
        qb = lax.dynamic_slice_in_dim(q, s0, Q_BLOCK, axis=1)
        cqb = lax.dynamic_slice_in_dim(c, s0, Q_BLOCK, axis=1)
        return attend(qb, k, v, cqb, c, s0 + jnp.arange(Q_BLOCK), kpos)

    o = lax.map(block, jnp.arange(S // Q_BLOCK))
    return jnp.transpose(o, (1, 0, 2, 3, 4)).reshape(B, S, D_ATTN)


def sample_attention(q, k, v, logf, cache_k, cache_v, cache_logf, page_table):
    DB, T = q.shape[:2]
    n_pages = page_table.shape[1]
    P = n_pages * PAGE_SIZE
    k_past = cache_k[page_table].reshape(DB, P, N_HEADS, HEAD_DIM)
    v_past = cache_v[page_table].reshape(DB, P, N_HEADS, HEAD_DIM)
    lf_past = cache_logf[page_table].reshape(DB, P, N_HEADS)
    k_all = jnp.concatenate([k_past, k.astype(k_past.dtype)], axis=1)
    v_all = jnp.concatenate([v_past, v.astype(v_past.dtype)], axis=1)
    c_all = jnp.cumsum(jnp.concatenate([lf_past.astype(jnp.float32), logf], axis=1), axis=1)
    o = attend(q, k_all, v_all, c_all[:, P:], c_all, P + jnp.arange(T), jnp.arange(P + T))
    return o.reshape(DB, T, D_ATTN)


def peer_chunk(xc, w_query, sub_keys, peer_u, peer_v):
    C = xc.shape[0]
    q = (xc @ w_query).reshape(C, PEER_HEADS, 2, D_KEY_HALF)
    s = jnp.einsum('chpd,hpnd->chpn', q.astype(jnp.float32), sub_keys.astype(jnp.float32))
    s1, i1 = lax.top_k(s[:, :, 0], PEER_TOPK)
    s2, i2 = lax.top_k(s[:, :, 1], PEER_TOPK)
    cand_s = (s1[..., :, None] + s2[..., None, :]).reshape(C, PEER_HEADS, PEER_TOPK * PEER_TOPK)
    cand_i = (i1[..., :, None] * N_KEYS + i2[..., None, :]).reshape(C, PEER_HEADS, PEER_TOPK * PEER_TOPK)
    top_s, top_pos = lax.top_k(cand_s, PEER_TOPK)
    idx = jnp.take_along_axis(cand_i, top_pos, axis=-1)
    g = jax.nn.softmax(top_s, axis=-1)
    u = peer_u[idx]
    a = jax.nn.gelu(jnp.einsum('cd,chkd->chk', xc, u).astype(jnp.float32), approximate=False)
    v = peer_v[idx]
    return jnp.einsum('chk,chkd->cd', (g * a).astype(v.dtype), v).astype(xc.dtype)


def peer_ffn(xn, w_query, sub_keys, peer_u, peer_v):
    n = xn.shape[0]
    pad = (-n) % PEER_CHUNK
    xp = jnp.pad(xn, ((0, pad), (0, 0))).reshape(-1, PEER_CHUNK, D_MODEL)
    out = lax.map(lambda xc: peer_chunk(xc, w_query, sub_keys, peer_u, peer_v), xp)
    return out.reshape(-1, D_MODEL)[:n]


def merge_and_ffn(x, pool_out, attn_out, gate_a, gate_b, w_branch_pool, w_branch_attn, w_out,
                  norm_ffn_g, w_query, sub_keys, peer_u, peer_v):
    mixed = jax.nn.sigmoid(gate_a) * (pool_out @ w_branch_pool) + jax.nn.sigmoid(gate_b) * (attn_out @ w_branch_attn)
    h = x + mixed @ w_out
    hn = rmsnorm(h, norm_ffn_g)
    return h + peer_ffn(hn.reshape(-1, D_MODEL), w_query, sub_keys, peer_u, peer_v).reshape(h.shape)


def layer_step(x_p, x_s, cache_k, cache_v, cache_logf, state_pool, page_table, norm_mix_g, w_in, b_forget,
               q_norm_g, k_norm_g, pool_mix_w, pool_scale, w_branch_pool, w_branch_attn, w_out, norm_ffn_g,
               w_query, sub_keys, peer_u, peer_v):
    u_p, q_p, k_p, v_p, lf_p, ga_p, gb_p = project_in(x_p, norm_mix_g, w_in, b_forget, q_norm_g, k_norm_g)
    hist0 = jnp.zeros((x_p.shape[0], POOL_HIST, D_POOL), u_p.dtype)
    pool_p, hist_p = pool_mixer(u_p, hist0, 0, pool_mix_w, pool_scale)
    attn_p = prompt_attention(q_p, k_p, v_p, lf_p)
    y_p = merge_and_ffn(x_p, pool_p, attn_p, ga_p, gb_p, w_branch_pool, w_branch_attn, w_out,
                        norm_ffn_g, w_query, sub_keys, peer_u, peer_v)
    u_s, q_s, k_s, v_s, lf_s, ga_s, gb_s = project_in(x_s, norm_mix_g, w_in, b_forget, q_norm_g, k_norm_g)
    pool_s, hist_s = pool_mixer(u_s, state_pool, PAST_LEN, pool_mix_w, pool_scale)
    attn_s = sample_attention(q_s, k_s, v_s, lf_s, cache_k, cache_v, cache_logf, page_table)
    y_s = merge_and_ffn(x_s, pool_s, attn_s, ga_s, gb_s, w_branch_pool, w_branch_attn, w_out,
                        norm_ffn_g, w_query, sub_keys, peer_u, peer_v)
    return y_p, y_s, k_p, v_p, lf_p, hist_p, k_s, v_s, lf_s, hist_s


def setup_inputs(seed: int = 0) -> dict:
    key = jax.random.key(seed)
    ks = jax.random.split(key, 24)
    nrm = jax.random.normal
    n_pages = PAST_LEN // PAGE_SIZE
    n_used = DEC_BATCH * n_pages
    n_phys = n_used + (n_used + 3) // 4
    page_table = jax.random.permutation(ks[0], n_phys)[:n_used].reshape(DEC_BATCH, n_pages).astype(jnp.int32)
    L = DEPTH
    return {
        'x_prompt': nrm(ks[1], (BATCH, SEQ, D_MODEL), jnp.float32),
        'x_sample': nrm(ks[2], (DEC_BATCH, DEC_SEQ, D_MODEL), jnp.float32),
        'cache_k': nrm(ks[3], (L, n_phys, PAGE_SIZE, N_HEADS, HEAD_DIM), jnp.float32),
        'cache_v': nrm(ks[4], (L, n_phys, PAGE_SIZE, N_HEADS, HEAD_DIM), jnp.float32),
        'cache_logf': jax.nn.log_sigmoid(3.0 + nrm(ks[5], (L, n_phys, PAGE_SIZE, N_HEADS), jnp.float32)),
        'state_pool': nrm(ks[6], (L, DEC_BATCH, POOL_HIST, D_POOL), jnp.float32),
        'page_table': page_table,
        'norm_mix_g': 1.0 + 0.02 * nrm(ks[7], (L, D_MODEL), jnp.float32),
        'w_in': nrm(ks[8], (L, D_MODEL, D_IN), jnp.float32) * D_MODEL ** -0.5,
        'b_forget': 3.0 + 0.5 * nrm(ks[9], (L, N_HEADS), jnp.float32),
        'q_norm_g': 1.0 + 0.02 * nrm(ks[10], (L, HEAD_DIM), jnp.float32),
        'k_norm_g': 1.0 + 0.02 * nrm(ks[11], (L, HEAD_DIM), jnp.float32),
        'pool_mix_w': nrm(ks[12], (L, POOL_GROUPS, POOL_GC, POOL_GC), jnp.float32) * POOL_GC ** -0.5,
        'pool_scale': 1.0 + 0.05 * nrm(ks[13], (L, D_POOL), jnp.float32),
        'w_branch_pool': nrm(ks[14], (L, D_POOL, D_MODEL), jnp.float32) * D_POOL ** -0.5,
        'w_branch_attn': nrm(ks[15], (L, D_ATTN, D_MODEL), jnp.float32) * D_ATTN ** -0.5,
        'w_out': nrm(ks[16], (L, D_MODEL, D_MODEL), jnp.float32) * D_MODEL ** -0.5,
        'norm_ffn_g': 1.0 + 0.02 * nrm(ks[17], (L, D_MODEL), jnp.float32),
        'w_query': nrm(ks[18], (L, D_MODEL, PEER_HEADS * D_KEY), jnp.float32) * D_MODEL ** -0.5,
        'sub_keys': nrm(ks[19], (L, PEER_HEADS, 2, N_KEYS, D_KEY_HALF), jnp.float32) * D_KEY_HALF ** -0.5,
        'peer_u': nrm(ks[20], (L, N_EXPERTS, D_MODEL), jnp.float32) * D_MODEL ** -0.5,
        'peer_v': nrm(ks[21], (L, N_EXPERTS, D_MODEL), jnp.float32) * (0.5 * PEER_HEADS ** -0.5),
    }


def reference(x_prompt, x_sample, cache_k, cache_v, cache_logf, state_pool, page_table, norm_mix_g, w_in,
              b_forget, q_norm_g, k_norm_g, pool_mix_w, pool_scale, w_branch_pool, w_branch_attn, w_out,
              norm_ffn_g, w_query, sub_keys, peer_u, peer_v):
    x_p, x_s = x_prompt, x_sample
    kp_l, vp_l, lfp_l, pp_l, ks_l, vs_l, lfs_l, ps_l = [], [], [], [], [], [], [], []
    for l in range(DEPTH):
        x_p, x_s, k_p, v_p, lf_p, h_p, k_s, v_s, lf_s, h_s = layer_step(
            x_p, x_s, cache_k[l], cache_v[l], cache_logf[l], state_pool[l], page_table,
            norm_mix_g[l], w_in[l], b_forget[l], q_norm_g[l], k_norm_g[l], pool_mix_w[l], pool_scale[l],
            w_branch_pool[l], w_branch_attn[l], w_out[l], norm_ffn_g[l], w_query[l], sub_keys[l],
            peer_u[l], peer_v[l])
        kp_l.append(k_p); vp_l.append(v_p); lfp_l.append(lf_p); pp_l.append(h_p)
        ks_l.append(k_s); vs_l.append(v_s); lfs_l.append(lf_s); ps_l.append(h_s)
    k_prompt = jnp.stack(kp_l); v_prompt = jnp.stack(vp_l)
    logf_prompt = jnp.stack(lfp_l); pool_prompt = jnp.stack(pp_l)
    k_sample = jnp.stack(ks_l); v_sample = jnp.stack(vs_l)
    logf_sample = jnp.stack(lfs_l); pool_sample = jnp.stack(ps_l)
    return (x_p, x_s, k_prompt, v_prompt, logf_prompt, pool_prompt, k_sample, v_sample, logf_sample, pool_sample)
```

```python
import functools

import numpy as np
import jax
import jax.numpy as jnp
from jax import lax
from jax.experimental import pallas as pl
from jax.experimental.pallas import tpu as pltpu

F32 = jnp.float32
BF16 = jnp.bfloat16

LANES = 128
SUBLANES = 8
VMEM_LIMIT = 56 * 1024 * 1024

HEAD_DIM = 64
POOL_WINDOWS = (2, 4, 8, 16)
POOL_HIST = 15
PAGE = 128
N_KEYS = 128
PEER_TOPK = 16
RMS_EPS = 1e-6
ATTN_SCALE = HEAD_DIM ** -0.5
HIGHEST = lax.Precision.HIGHEST


def _params(sem, vmem=VMEM_LIMIT):
    return pltpu.CompilerParams(dimension_semantics=sem, vmem_limit_bytes=vmem)


def _full(shape):
    n = len(shape)
    return pl.BlockSpec(shape, lambda *_: (0,) * n)


def _log_sigmoid(z):
    return jnp.minimum(z, 0.0) - jnp.log1p(jnp.exp(-jnp.abs(z)))


def _group_mean_sq(x, gmat):
    xx = x * x
    hi = xx.astype(BF16)
    lo = (xx - hi.astype(F32)).astype(BF16)
    s = jnp.dot(hi, gmat, preferred_element_type=F32) + jnp.dot(lo, gmat, preferred_element_type=F32)
    return s * (1.0 / HEAD_DIM)


def _proj_kernel(x_ref, g_ref, w_ref, bf_ref, qg_ref, kg_ref, gm_ref,
                 u_ref, qb_ref, k_ref, kb_ref, v_ref, vb_ref, lf_ref):
    x = x_ref[...]
    ms = jnp.mean(x * x, axis=-1, keepdims=True)
    xn = (x * lax.rsqrt(ms + RMS_EPS) * g_ref[...]).astype(BF16)
    p = jnp.dot(xn, w_ref[...], preferred_element_type=F32)
    d = u_ref.shape[-1]
    u = p[:, 0:d]
    q = p[:, d:2 * d]
    k = p[:, 2 * d:3 * d]
    v = p[:, 3 * d:4 * d]
    f = p[:, 4 * d:4 * d + LANES]
    gm = gm_ref[...]
    qn = q * lax.rsqrt(_group_mean_sq(q, gm) + RMS_EPS) * qg_ref[...]
    kn = k * lax.rsqrt(_group_mean_sq(k, gm) + RMS_EPS) * kg_ref[...]
    u_ref[...] = u
    qb_ref[...] = (qn * ATTN_SCALE).astype(BF16)
    k_ref[...] = kn
    kb_ref[...] = kn.astype(BF16)
    v_ref[...] = v
    vb_ref[...] = v.astype(BF16)
    lf = _log_sigmoid(f + bf_ref[...])
    lf_ref[...] = lf[:, 0:lf_ref.shape[-1]]


def _project(x, g_mix, w_a, b_f, qg, kg, gmat, n_heads, tm):
    n, dm = x.shape
    d = n_heads * HEAD_DIM
    row = lambda w: pl.BlockSpec((tm, w), lambda i: (i, 0))
    outs = (
        jax.ShapeDtypeStruct((n, d), F32),
        jax.ShapeDtypeStruct((n, d), BF16),
        jax.ShapeDtypeStruct((n, d), F32),
        jax.ShapeDtypeStruct((n, d), BF16),
        jax.ShapeDtypeStruct((n, d), F32),
        jax.ShapeDtypeStruct((n, d), BF16),
        jax.ShapeDtypeStruct((n, n_heads), F32),
    )
    return pl.pallas_call(
        _proj_kernel,
        grid=(n // tm,),
        in_specs=[row(dm), _full(g_mix.shape), _full(w_a.shape), _full(b_f.shape),
                  _full(qg.shape), _full(kg.shape), _full(gmat.shape)],
        out_specs=(row(d), row(d), row(d), row(d), row(d), row(d), row(n_heads)),
        out_shape=outs,
        compiler_params=_params(("arbitrary",)),
        name="proj_in",
    )(x, g_mix, w_a, b_f, qg, kg, gmat)


def _pool_groups(u_g, win_sum, cnt, w_ref, sc_ref, g):
    m = win_sum / cnt - u_g
    o = jnp.dot(m.astype(BF16), w_ref[g], preferred_element_type=F32)
    return o * sc_ref[:, g * LANES:(g + 1) * LANES]


def _pool_prompt_kernel(u_ref, w_ref, sc_ref, o_ref, hist_ref, ext_ref, *, tm):
    si = pl.program_id(1)
    hp = POOL_HIST + 1

    @pl.when(si == 0)
    def _():
        ext_ref[0:hp, :] = jnp.zeros((hp, ext_ref.shape[1]), F32)

    u = u_ref[0]
    ext_ref[hp:hp + tm, :] = u
    pos = si * tm + lax.broadcasted_iota(jnp.int32, (tm, 1), 0)
    for g, w in enumerate(POOL_WINDOWS):
        ls = slice(g * LANES, (g + 1) * LANES)
        u_g = u[:, ls]
        acc = u_g
        for dlt in range(1, w):
            acc = acc + ext_ref[hp - dlt:hp - dlt + tm, ls]
        cnt = jnp.minimum(pos + 1, w).astype(F32)
        o_ref[0, :, ls] = _pool_groups(u_g, acc, cnt, w_ref, sc_ref, g).astype(o_ref.dtype)
    tail = ext_ref[tm:tm + hp, :]
    hist_ref[0] = tail
    ext_ref[0:hp, :] = tail


def _pool_prompt(u3, w_mix, scale, tm):
    b, s, d = u3.shape
    hp = POOL_HIST + 1
    return pl.pallas_call(
        functools.partial(_pool_prompt_kernel, tm=tm),
        grid=(b, s // tm),
        in_specs=[pl.BlockSpec((1, tm, d), lambda bi, si: (bi, si, 0)),
                  _full(w_mix.shape), _full(scale.shape)],
        out_specs=(pl.BlockSpec((1, tm, d), lambda bi, si: (bi, si, 0)),
                   pl.BlockSpec((1, hp, d), lambda bi, si: (bi, 0, 0))),
        out_shape=(jax.ShapeDtypeStruct((b, s, d), BF16),
                   jax.ShapeDtypeStruct((b, hp, d), F32)),
        scratch_shapes=[pltpu.VMEM((hp + tm, d), F32)],
        compiler_params=_params(("arbitrary", "arbitrary")),
        name="pool_prompt",
    )(u3, w_mix, scale)


def _pool_sample_kernel(u_ref, st_ref, w_ref, sc_ref, o_ref, ext_ref, *, pos0):
    nb, t, d = u_ref.shape
    hp = POOL_HIST + 1
    ext_ref[:, 0:hp, :] = st_ref[...]
    ext_ref[:, hp:hp + t, :] = u_ref[...]
    pos = pos0 + lax.broadcasted_iota(jnp.int32, (1, t, 1), 1)
    for g, w in enumerate(POOL_WINDOWS):
        ls = slice(g * LANES, (g + 1) * LANES)
        u_g = u_ref[:, :, ls]
        acc = u_g
        for dlt in range(1, w):
            acc = acc + ext_ref[:, hp - dlt:hp - dlt + t, ls]
        cnt = jnp.minimum(pos + 1, w).astype(F32)
        m = (acc / cnt - u_g).reshape(nb * t, LANES)
        o = jnp.dot(m.astype(BF16), w_ref[g], preferred_element_type=F32) * sc_ref[:, ls]
        o_ref[:, ls] = o.astype(o_ref.dtype)


def _pool_sample(u3, state16, w_mix, scale, pos0):
    nb, t, d = u3.shape
    hp = POOL_HIST + 1
    return pl.pallas_call(
        functools.partial(_pool_sample_kernel, pos0=pos0),
        grid=(1,),
        in_specs=[_full(u3.shape), _full(state16.shape), _full(w_mix.shape), _full(scale.shape)],
        out_specs=_full((nb * t, d)),
        out_shape=jax.ShapeDtypeStruct((nb * t, d), BF16),
        scratch_shapes=[pltpu.VMEM((nb, hp + t, d), F32)],
        compiler_params=_params(("arbitrary",)),
        name="pool_sample",
    )(u3, state16, w_mix, scale)


def _upper_tri(n):
    r = lax.broadcasted_iota(jnp.int32, (n, n), 0)
    c = lax.broadcasted_iota(jnp.int32, (n, n), 1)
    return (r <= c).astype(F32)


def _cumsum_prompt_kernel(lf_ref, c_ref, carry_ref):
    @pl.when(pl.program_id(1) == 0)
    def _():
        carry_ref[...] = jnp.zeros_like(carry_ref)

    tc = lf_ref.shape[-1]
    c = carry_ref[...] + jnp.dot(lf_ref[0], _upper_tri(tc), precision=HIGHEST, preferred_element_type=F32)
    c_ref[0] = c
    carry_ref[...] = jnp.broadcast_to(c[:, tc - 1:tc], c.shape)


def _cumsum_prompt(lf_t, tc):
    b, h, s = lf_t.shape
    return pl.pallas_call(
        _cumsum_prompt_kernel,
        grid=(b, s // tc),
        in_specs=[pl.BlockSpec((1, h, tc), lambda bi, si: (bi, 0, si))],
        out_specs=pl.BlockSpec((1, h, tc), lambda bi, si: (bi, 0, si)),
        out_shape=jax.ShapeDtypeStruct((b, h, s), F32),
        scratch_shapes=[pltpu.VMEM((h, tc), F32)],
        compiler_params=_params(("arbitrary", "arbitrary")),
        name="cumsum_prompt",
    )(lf_t)


def _cumsum_sample_kernel(pt_ref, lfp_ref, lfn_ref, c_ref, carry_ref, *, n_pages):
    j = pl.program_id(1)

    @pl.when(j == 0)
    def _():
        carry_ref[...] = jnp.zeros_like(carry_ref)

    tri = _upper_tri(PAGE)

    @pl.when(j < n_pages)
    def _():
        c = carry_ref[...] + jnp.dot(lfp_ref[0], tri, precision=HIGHEST, preferred_element_type=F32)
        c_ref[0] = c
        carry_ref[...] = jnp.broadcast_to(c[:, PAGE - 1:PAGE], c.shape)

    @pl.when(j == n_pages)
    def _():
        c_ref[0] = carry_ref[...] + jnp.dot(lfn_ref[0], tri, precision=HIGHEST, preferred_element_type=F32)


def _cumsum_sample(page_table, lf_cache_t, lf_new_t):
    db, n_pages = page_table.shape
    h = lf_cache_t.shape[1]
    grid_spec = pltpu.PrefetchScalarGridSpec(
        num_scalar_prefetch=1,
        grid=(db, n_pages + 1),
        in_specs=[pl.BlockSpec((1, h, PAGE), lambda b, j, pt: (pt[b, jnp.minimum(j, n_pages - 1)], 0, 0)),
                  pl.BlockSpec((1, h, PAGE), lambda b, j, pt: (b, 0, 0))],
        out_specs=pl.BlockSpec((1, h, PAGE), lambda b, j, pt: (b, 0, j)),
        scratch_shapes=[pltpu.VMEM((h, PAGE), F32)],
    )
    return pl.pallas_call(
        functools.partial(_cumsum_sample_kernel, n_pages=n_pages),
        grid_spec=grid_spec,
        out_shape=jax.ShapeDtypeStruct((db, h, (n_pages + 1) * PAGE), F32),
        compiler_params=_params(("arbitrary", "arbitrary")),
        name="cumsum_sample",
    )(page_table, lf_cache_t, lf_new_t)


def _attn_prompt_kernel(q_ref, k_ref, v_ref, cq_ref, ck_ref, o_ref, m_ref, l_ref, acc_ref, *, tq, tk):
    qi = pl.program_id(2)
    ki = pl.program_id(3)
    nk = pl.num_programs(3)

    @pl.when(ki == 0)
    def _():
        m_ref[...] = jnp.full_like(m_ref, -jnp.inf)
        l_ref[...] = jnp.zeros_like(l_ref)
        acc_ref[...] = jnp.zeros_like(acc_ref)

    @pl.when(ki <= qi)
    def _():
        q = q_ref[0]
        k = k_ref[0]
        v = v_ref[0]
        lane = lax.broadcasted_iota(jnp.int32, (1, LANES), 1)
        qpos = qi * tq + lax.broadcasted_iota(jnp.int32, (tq, tk), 0)
        kpos = ki * tk + lax.broadcasted_iota(jnp.int32, (tq, tk), 1)
        visible = kpos <= qpos
        for hh in range(2):
            in_head = (lane >= hh * HEAD_DIM) & (lane < (hh + 1) * HEAD_DIM)
            qm = jnp.where(in_head, q, jnp.zeros_like(q))
            s = lax.dot_general(qm, k, (((1,), (1,)), ((), ())), preferred_element_type=F32)
            s = s + (cq_ref[0, 0, :, hh:hh + 1] - ck_ref[0, 0, hh:hh + 1, :])
            s = jnp.where(visible, s, -jnp.inf)
            m_prev = m_ref[hh]
            m_new = jnp.maximum(m_prev, jnp.max(s, axis=1, keepdims=True))
            alpha = jnp.exp(m_prev - m_new)
            p = jnp.exp(s - m_new)
            l_ref[hh] = alpha * l_ref[hh] + jnp.sum(p, axis=1, keepdims=True)
            acc_ref[hh] = alpha * acc_ref[hh] + jnp.dot(p.astype(BF16), v, preferred_element_type=F32)
            m_ref[hh] = m_new

    @pl.when(ki == nk - 1)
    def _():
        lane = lax.broadcasted_iota(jnp.int32, (1, LANES), 1)
        o0 = acc_ref[0] / l_ref[0]
        o1 = acc_ref[1] / l_ref[1]
        o_ref[0] = jnp.where(lane < HEAD_DIM, o0, o1).astype(o_ref.dtype)


def _attn_prompt(qb, kb, vb, cq, ck, tq, tk):
    b, s, d = qb.shape
    npair = d // LANES
    return pl.pallas_call(
        functools.partial(_attn_prompt_kernel, tq=tq, tk=tk),
        grid=(b, npair, s // tq, s // tk),
        in_specs=[
            pl.BlockSpec((1, tq, LANES), lambda bi, p, qi, ki: (bi, qi, p)),
            pl.BlockSpec((1, tk, LANES), lambda bi, p, qi, ki: (bi, jnp.minimum(ki, qi), p)),
            pl.BlockSpec((1, tk, LANES), lambda bi, p, qi, ki: (bi, jnp.minimum(ki, qi), p)),
            pl.BlockSpec((1, 1, tq, 2), lambda bi, p, qi, ki: (bi, p, qi, 0)),
            pl.BlockSpec((1, 1, 2, tk), lambda bi, p, qi, ki: (bi, p, 0, jnp.minimum(ki, qi))),
        ],
        out_specs=pl.BlockSpec((1, tq, LANES), lambda bi, p, qi, ki: (bi, qi, p)),
        out_shape=jax.ShapeDtypeStruct((b, s, d), BF16),
        scratch_shapes=[pltpu.VMEM((2, tq, 1), F32), pltpu.VMEM((2, tq, 1), F32),
                        pltpu.VMEM((2, tq, LANES), F32)],
        compiler_params=_params(("arbitrary",) * 4),
        name="attn_prompt",
    )(qb, kb, vb, cq, ck)


def _attn_sample_kernel(pt_ref, q_ref, kc_ref, vc_ref, kn_ref, vn_ref, ck_ref, cn_ref, o_ref,
                        qbd_ref, cq_ref, m_ref, l_ref, acc_ref, *, n_pages, n_heads, t_new):
    j = pl.program_id(1)
    d = n_heads * HEAD_DIM
    rows = t_new * n_heads
    head_of_lane = lax.broadcasted_iota(jnp.int32, (n_heads, d), 1) // HEAD_DIM
    head_mask = head_of_lane == lax.broadcasted_iota(jnp.int32, (n_heads, d), 0)

    @pl.when(j == 0)
    def _():
        q = q_ref[0].astype(F32)
        cn = cn_ref[0]
        for t in range(t_new):
            qt = jnp.broadcast_to(q[t:t + 1, :], (n_heads, d))
            qbd_ref[t * n_heads:(t + 1) * n_heads, :] = jnp.where(head_mask, qt, jnp.zeros_like(qt))
            cq_ref[t * n_heads:(t + 1) * n_heads, :] = jnp.broadcast_to(cn[:, t:t + 1], (n_heads, PAGE))
        m_ref[...] = jnp.full_like(m_ref, -jnp.inf)
        l_ref[...] = jnp.zeros_like(l_ref)
        acc_ref[...] = jnp.zeros_like(acc_ref)

    def step(kb, vb, visible):
        s = lax.dot_general(qbd_ref[...].astype(BF16), kb, (((1,), (1,)), ((), ())),
                            preferred_element_type=F32)
        ck = ck_ref[0]
        s = s + (cq_ref[...] - jnp.concatenate([ck] * t_new, axis=0))
        if visible is not None:
            s = jnp.where(visible, s, -jnp.inf)
        m_prev = m_ref[...]
        m_new = jnp.maximum(m_prev, jnp.max(s, axis=1, keepdims=True))
        alpha = jnp.exp(m_prev - m_new)
        p = jnp.exp(s - m_new)
        l_ref[...] = alpha * l_ref[...] + jnp.sum(p, axis=1, keepdims=True)
        acc_ref[...] = alpha * acc_ref[...] + jnp.dot(p.astype(BF16), vb, preferred_element_type=F32)
        m_ref[...] = m_new

    @pl.when(j < n_pages)
    def _():
        step(kc_ref[0].astype(BF16), vc_ref[0].astype(BF16), None)

    @pl.when(j == n_pages)
    def _():
        pad = jnp.zeros((PAGE - t_new, d), F32)
        kb = jnp.concatenate([kn_ref[0], pad], axis=0).astype(BF16)
        vb = jnp.concatenate([vn_ref[0], pad], axis=0).astype(BF16)
        key = lax.broadcasted_iota(jnp.int32, (rows, PAGE), 1)
        tok = lax.broadcasted_iota(jnp.int32, (rows, PAGE), 0) // n_heads
        step(kb, vb, key <= tok)
        o = acc_ref[...] / l_ref[...]
        out_rows = []
        for t in range(t_new):
            ot = jnp.where(head_mask, o[t * n_heads:(t + 1) * n_heads, :], 0.0)
            out_rows.append(jnp.sum(ot, axis=0, keepdims=True))
        o_ref[0] = jnp.concatenate(out_rows, axis=0)


def _attn_sample(page_table, qb, k_cache, v_cache, k_new, v_new, c_all, n_heads):
    db, t_new, d = qb.shape
    n_pages = page_table.shape[1]
    rows = t_new * n_heads
    page_idx = lambda b, j, pt: (pt[b, jnp.minimum(j, n_pages - 1)], 0, 0)
    per_seq = lambda b, j, pt: (b, 0, 0)
    grid_spec = pltpu.PrefetchScalarGridSpec(
        num_scalar_prefetch=1,
        grid=(db, n_pages + 1),
        in_specs=[
            pl.BlockSpec((1, t_new, d), per_seq),
            pl.BlockSpec((1, PAGE, d), page_idx),
            pl.BlockSpec((1, PAGE, d), page_idx),
            pl.BlockSpec((1, t_new, d), per_seq),
            pl.BlockSpec((1, t_new, d), per_seq),
            pl.BlockSpec((1, n_heads, PAGE), lambda b, j, pt: (b, 0, j)),
            pl.BlockSpec((1, n_heads, PAGE), lambda b, j, pt: (b, 0, n_pages)),
        ],
        out_specs=pl.BlockSpec((1, t_new, d), per_seq),
        scratch_shapes=[pltpu.VMEM((rows, d), F32), pltpu.VMEM((rows, PAGE), F32),
                        pltpu.VMEM((rows, 1), F32), pltpu.VMEM((rows, 1), F32),
                        pltpu.VMEM((rows, d), F32)],
    )
    return pl.pallas_call(
        functools.partial(_attn_sample_kernel, n_pages=n_pages, n_heads=n_heads, t_new=t_new),
        grid_spec=grid_spec,
        out_shape=jax.ShapeDtypeStruct((db, t_new, d), F32),
        compiler_params=_params(("arbitrary", "arbitrary")),
        name="attn_sample",
    )(page_table, qb, k_cache, v_cache, k_new, v_new, c_all, c_all)


def _merge_kernel(x_ref, pool_ref, attn_ref, gmix_ref, wg_ref, wbp_ref, wba_ref, wo_ref, gffn_ref, wq_ref,
                  h_ref, hnt_ref, qp_ref):
    x = x_ref[...]
    dm = x.shape[1]
    ms = jnp.mean(x * x, axis=-1, keepdims=True)
    xn = (x * lax.rsqrt(ms + RMS_EPS) * gmix_ref[...]).astype(BF16)
    gates = jax.nn.sigmoid(jnp.dot(xn, wg_ref[...], preferred_element_type=F32))
    mixed = (gates[:, 0:dm] * jnp.dot(pool_ref[...], wbp_ref[...], preferred_element_type=F32)
             + gates[:, dm:2 * dm] * jnp.dot(attn_ref[...], wba_ref[...], preferred_element_type=F32))
    h = x + jnp.dot(mixed.astype(BF16), wo_ref[...], preferred_element_type=F32)
    h_ref[...] = h
    hms = jnp.mean(h * h, axis=-1, keepdims=True)
    hn = h * lax.rsqrt(hms + RMS_EPS) * gffn_ref[...]
    hnb = hn.astype(BF16)
    hnt_ref[...] = hn.T.astype(BF16)
    qp_ref[...] = jnp.dot(hnb, wq_ref[...], preferred_element_type=F32).astype(BF16)


def _merge(x, pool, attn, g_mix, w_gates, wbp, wba, wo, g_ffn, wq, tm):
    n, dm = x.shape
    dq = wq.shape[1]
    row = lambda w: pl.BlockSpec((tm, w), lambda i: (i, 0))
    return pl.pallas_call(
        _merge_kernel,
        grid=(n // tm,),
        in_specs=[row(dm), row(pool.shape[1]), row(attn.shape[1]), _full(g_mix.shape), _full(w_gates.shape),
                  _full(wbp.shape), _full(wba.shape), _full(wo.shape), _full(g_ffn.shape), _full(wq.shape)],
        out_specs=(row(dm), pl.BlockSpec((dm, tm), lambda i: (0, i)), row(dq)),
        out_shape=(jax.ShapeDtypeStruct((n, dm), F32),
                   jax.ShapeDtypeStruct((dm, n), BF16),
                   jax.ShapeDtypeStruct((n, dq), BF16)),
        compiler_params=_params(("arbitrary",)),
        name="merge",
    )(x, pool, attn, g_mix, w_gates, wbp, wba, wo, g_ffn, wq)


def _oddeven_merge_sort_pairs(n):
    pairs = []
    p = 1
    while p < n:
        k = p
        while k >= 1:
            for j in range(k % p, n - k, 2 * k):
                for i in range(min(k, n - j - k)):
                    if (i + j) // (2 * p) == (i + j + k) // (2 * p):
                        pairs.append((i + j, i + j + k))
            k //= 2
        p *= 2
    return pairs


def _bitonic_merge_pairs(n):
    pairs = []
    k = n // 2
    while k >= 1:
        for i in range(n):
            if (i // k) % 2 == 0 and i + k < n:
                pairs.append((i, i + k))
        k //= 2
    return pairs


_SORT16 = _oddeven_merge_sort_pairs(16)
_MERGE16 = _bitonic_merge_pairs(16)


def _apply_desc(vals, pairs):
    vals = list(vals)
    for a, b in pairs:
        hi = jnp.maximum(vals[a], vals[b])
        lo = jnp.minimum(vals[a], vals[b])
        vals[a], vals[b] = hi, lo
    return vals


def _top16_across_sublanes(vals):
    for shift in (4, 2, 1):
        other = [pltpu.roll(v, shift, axis=0) for v in vals]
        vals = [jnp.maximum(vals[i], other[PEER_TOPK - 1 - i]) for i in range(PEER_TOPK)]
        vals = _apply_desc(vals, _MERGE16)
    return vals


def _topk_tile(s_tile):
    vals = [s_tile[SUBLANES * i:SUBLANES * (i + 1), :] for i in range(N_KEYS // SUBLANES)]
    vals = _apply_desc(vals, _SORT16)
    return _top16_across_sublanes(vals)


def _route_kernel(qp_ref, sk_ref, s1_ref, s2_ref, st_ref, *, n_heads, lt):
    tn = qp_ref.shape[0]
    neg = -jnp.inf
    sub = lax.broadcasted_iota(jnp.int32, (SUBLANES, lt), 0)
    for h in range(n_heads):
        for half, dst in ((0, s1_ref), (1, s2_ref)):
            c0 = (2 * h + half) * LANES
            dst[h] = lax.dot_general(sk_ref[2 * h + half], qp_ref[:, c0:c0 + LANES],
                                     (((1,), (1,)), ((), ())), preferred_element_type=F32)

    def body(it, carry):
        h = it // (tn // lt)
        l0 = pl.multiple_of((it % (tn // lt)) * lt, lt)
        a = _topk_tile(s1_ref[h, :, pl.ds(l0, lt)])
        b = _topk_tile(s2_ref[h, :, pl.ds(l0, lt)])
        def pack(vs):
            out = vs[0]
            for i in range(1, len(vs)):
                out = jnp.where(sub == i, vs[i], out)
            return out
        b_lo, b_hi, a_hi = pack(b[0:8]), pack(b[8:16]), pack(a[8:16])
        cands = [a[0] + b_lo, a[0] + b_hi, a[1] + b_lo]
        for k in range(2, 8):
            cands.append(jnp.where(sub < PEER_TOPK // (k + 1), a[k] + b_lo, neg))
        cands.append(a_hi + b[0])
        cands = cands + [jnp.full((SUBLANES, lt), neg, F32)] * (PEER_TOPK - len(cands))
        top = _top16_across_sublanes(_apply_desc(cands, _SORT16))
        z = jnp.zeros((SUBLANES, lt), F32)
        for i in range(PEER_TOPK):
            z = z + jnp.exp(top[i] - top[0])
        row = lambda v: v[0:1, :]
        st_ref[0, h, :, pl.ds(l0, lt)] = row(top[PEER_TOPK - 1])
        st_ref[1, h, :, pl.ds(l0, lt)] = row(a[0])
        st_ref[2, h, :, pl.ds(l0, lt)] = row(b[0])
        st_ref[3, h, :, pl.ds(l0, lt)] = row(1.0 / z)
        return carry

    lax.fori_loop(0, n_heads * (tn // lt), body, 0)


def _route(qp, sub_keys_b, n_heads, tn, lt=LANES):
    n, dq = qp.shape
    return pl.pallas_call(
        functools.partial(_route_kernel, n_heads=n_heads, lt=lt),
        grid=(n // tn,),
        in_specs=[pl.BlockSpec((tn, dq), lambda i: (i, 0)), _full(sub_keys_b.shape)],
        out_specs=(pl.BlockSpec((n_heads, N_KEYS, tn), lambda i: (0, 0, i)),
                   pl.BlockSpec((n_heads, N_KEYS, tn), lambda i: (0, 0, i)),
                   pl.BlockSpec((4, n_heads, 1, tn), lambda i: (0, 0, 0, i))),
        out_shape=(jax.ShapeDtypeStruct((n_heads, N_KEYS, n), F32),
                   jax.ShapeDtypeStruct((n_heads, N_KEYS, n), F32),
                   jax.ShapeDtypeStruct((4, n_heads, 1, n), F32)),
        compiler_params=_params(("arbitrary",)),
        name="peer_route",
    )(qp, sub_keys_b)


def _peer_kernel(hnt_ref, s1_ref, s2_ref, st_ref, u_ref, vt_ref, h_ref, y_ref,
                 p1_ref, e2_ref, s1b_ref, p1b_ref, a_ref, wg_ref, acc_ref, *, n_heads, rows_per_step):
    e = pl.program_id(1)
    ne = pl.num_programs(1)
    tt = hnt_ref.shape[1]
    n_lt = tt // LANES
    n_jv = N_KEYS // SUBLANES

    @pl.when(e == 0)
    def _():
        acc_ref[...] = jnp.zeros_like(acc_ref)
        for h in range(n_heads):
            p1_ref[h] = jnp.exp(s1_ref[h] - st_ref[1, h]) * st_ref[3, h]
            e2_ref[h] = jnp.exp(s2_ref[h] - st_ref[2, h])

    a_ref[...] = jnp.dot(u_ref[...], hnt_ref[...], preferred_element_type=F32)

    i0 = pl.multiple_of(e * rows_per_step, SUBLANES)
    for h in range(n_heads):
        s1_rows = s1_ref[h, pl.ds(i0, rows_per_step), :]
        p1_rows = p1_ref[h, pl.ds(i0, rows_per_step), :]
        for r in range(rows_per_step):
            s1b_ref[h, r] = jnp.broadcast_to(s1_rows[r:r + 1, :], (SUBLANES, tt))
            p1b_ref[h, r] = jnp.broadcast_to(p1_rows[r:r + 1, :], (SUBLANES, tt))

    def body(it, carry):
        tl = it // n_jv
        jv = it % n_jv
        l0 = pl.multiple_of(tl * LANES, LANES)
        j0 = pl.multiple_of(jv * SUBLANES, SUBLANES)
        lanes = pl.ds(l0, LANES)
        s2 = [s2_ref[h, pl.ds(j0, SUBLANES), lanes] for h in range(n_heads)]
        e2 = [e2_ref[h, pl.ds(j0, SUBLANES), lanes] for h in range(n_heads)]
        bcast = lambda row: jnp.broadcast_to(row, (SUBLANES, LANES))
        thr = [bcast(st_ref[0, h, :, lanes]) for h in range(n_heads)]
        for r in range(rows_per_step):
            w = jnp.zeros((SUBLANES, LANES), F32)
            for h in range(n_heads):
                w = jnp.where(s2[h] + s1b_ref[h, r, :, lanes] >= thr[h], w + e2[h] * p1b_ref[h, r, :, lanes], w)
            r0 = pl.multiple_of(r * N_KEYS + j0, SUBLANES)
            act = a_ref[pl.ds(r0, SUBLANES), lanes]
            gelu = 0.5 * act * (1.0 + lax.erf(act * np.float32(1.0 / np.sqrt(2.0))))
            wg_ref[pl.ds(r0, SUBLANES), lanes] = w * gelu
        return carry

    lax.fori_loop(0, n_lt * n_jv, body, 0)
    acc_ref[...] += jnp.dot(vt_ref[...], wg_ref[...].astype(BF16), preferred_element_type=F32)

    @pl.when(e == ne - 1)
    def _():
        y_ref[...] = h_ref[...] + acc_ref[...].T


def _peer(hnt, s1, s2, stats, u_b, vt_b, h, tt, rows_per_step):
    dm, n = hnt.shape
    n_heads = s1.shape[0]
    n_exp = u_b.shape[0]
    eb = rows_per_step * N_KEYS
    return pl.pallas_call(
        functools.partial(_peer_kernel, n_heads=n_heads, rows_per_step=rows_per_step),
        grid=(n // tt, n_exp // eb),
        in_specs=[
            pl.BlockSpec((dm, tt), lambda t, e: (0, t)),
            pl.BlockSpec((n_heads, N_KEYS, tt), lambda t, e: (0, 0, t)),
            pl.BlockSpec((n_heads, N_KEYS, tt), lambda t, e: (0, 0, t)),
            pl.BlockSpec((4, n_heads, 1, tt), lambda t, e: (0, 0, 0, t)),
            pl.BlockSpec((eb, dm), lambda t, e: (e, 0)),
            pl.BlockSpec((dm, eb), lambda t, e: (0, e)),
            pl.BlockSpec((tt, dm), lambda t, e: (t, 0)),
        ],
        out_specs=pl.BlockSpec((tt, dm), lambda t, e: (t, 0)),
        out_shape=jax.ShapeDtypeStruct((n, dm), F32),
        scratch_shapes=[pltpu.VMEM((n_heads, N_KEYS, tt), F32), pltpu.VMEM((n_heads, N_KEYS, tt), F32),
                        pltpu.VMEM((n_heads, rows_per_step, SUBLANES, tt), F32),
                        pltpu.VMEM((n_heads, rows_per_step, SUBLANES, tt), F32),
                        pltpu.VMEM((eb, tt), F32), pltpu.VMEM((eb, tt), F32), pltpu.VMEM((dm, tt), F32)],
        compiler_params=_params(("arbitrary", "arbitrary")),
        name="peer_dense",
    )(hnt, s1, s2, stats, u_b, vt_b, h)


def _layer(x_p, x_s, cache_k, cache_v, cache_logf, state_pool, page_table, norm_mix_g, w_in, b_forget,
           q_norm_g, k_norm_g, pool_mix_w, pool_scale, w_branch_pool, w_branch_attn, w_out, norm_ffn_g,
           w_query, sub_keys, peer_u, peer_v):
    b, s, dm = x_p.shape
    db, t_new, _ = x_s.shape
    n_heads = b_forget.shape[0]
    d = n_heads * HEAD_DIM
    n_p, n_s = b * s, db * t_new
    n = n_p + n_s
    past_len = page_table.shape[1] * PAGE
    tm = 512

    w_a = jnp.concatenate([w_in[:, 0:4 * d + n_heads],
                           jnp.zeros((dm, LANES - n_heads), w_in.dtype)], axis=1).astype(BF16)
    w_gates = w_in[:, 4 * d + n_heads:].astype(BF16)
    b_f = jnp.concatenate([b_forget, jnp.zeros((LANES - n_heads,), F32)]).reshape(1, LANES)
    qg = jnp.tile(q_norm_g, n_heads).reshape(1, d)
    kg = jnp.tile(k_norm_g, n_heads).reshape(1, d)
    lane_head = np.arange(d) // HEAD_DIM
    gmat = jnp.asarray(lane_head[:, None] == lane_head[None, :], BF16)
    g_mix = norm_mix_g.reshape(1, dm)
    g_ffn = norm_ffn_g.reshape(1, dm)
    w_mix = pool_mix_w.astype(BF16)
    p_scale = pool_scale.reshape(1, d)

    x_all = jnp.concatenate([x_p.reshape(n_p, dm), x_s.reshape(n_s, dm)], axis=0)
    u, qb, k, kb, v, vb, lf = _project(x_all, g_mix, w_a, b_f, qg, kg, gmat, n_heads, tm)

    pool_p, hist16 = _pool_prompt(u[:n_p].reshape(b, s, d), w_mix, p_scale, tm)
    state16 = jnp.concatenate([jnp.zeros((db, 1, d), F32), state_pool], axis=1)
    u_s = u[n_p:].reshape(db, t_new, d)
    pool_s = _pool_sample(u_s, state16, w_mix, p_scale, past_len)
    hist_p = hist16[:, 1:, :]
    hist_s = jnp.concatenate([state_pool, u_s], axis=1)[:, -POOL_HIST:, :]

    lf_p = lf[:n_p].reshape(b, s, n_heads)
    c_t = _cumsum_prompt(jnp.transpose(lf_p, (0, 2, 1)), 512)
    ck = c_t.reshape(b, n_heads // 2, 2, s)
    cq = jnp.transpose(ck, (0, 1, 3, 2))
    attn_p = _attn_prompt(qb[:n_p].reshape(b, s, d), kb[:n_p].reshape(b, s, d), vb[:n_p].reshape(b, s, d),
                          cq, ck, 512, 512)

    lf_s = lf[n_p:].reshape(db, t_new, n_heads)
    lf_new_t = jnp.concatenate([jnp.transpose(lf_s, (0, 2, 1)),
                                jnp.zeros((db, n_heads, PAGE - t_new), F32)], axis=2)
    lf_cache_t = jnp.transpose(cache_logf, (0, 2, 1))
    c_all = _cumsum_sample(page_table, lf_cache_t, lf_new_t)
    n_phys = cache_k.shape[0]
    attn_s = _attn_sample(page_table, qb[n_p:].reshape(db, t_new, d),
                          cache_k.reshape(n_phys, PAGE, d), cache_v.reshape(n_phys, PAGE, d),
                          k[n_p:].reshape(db, t_new, d), v[n_p:].reshape(db, t_new, d), c_all, n_heads)

    pool_all = jnp.concatenate([pool_p.reshape(n_p, d), pool_s], axis=0)
    attn_all = jnp.concatenate([attn_p.reshape(n_p, d), attn_s.reshape(n_s, d).astype(BF16)], axis=0)
    h, hnt, qp = _merge(x_all, pool_all, attn_all, g_mix, w_gates, w_branch_pool.astype(BF16),
                        w_branch_attn.astype(BF16), w_out.astype(BF16), g_ffn, w_query.astype(BF16), tm)
    n_ph = sub_keys.shape[0]
    sk_b = sub_keys.reshape(n_ph * 2, N_KEYS, sub_keys.shape[-1]).astype(BF16)
    s1, s2, stats = _route(qp, sk_b, n_ph, 512)
    y = _peer(hnt, s1, s2, stats, peer_u.astype(BF16), jnp.transpose(peer_v).astype(BF16), h, 512, 8)

    y_p = y[:n_p].reshape(b, s, dm)
    y_s = y[n_p:].reshape(db, t_new, dm)
    k_p = k[:n_p].reshape(b, s, n_heads, HEAD_DIM)
    v_p = v[:n_p].reshape(b, s, n_heads, HEAD_DIM)
    k_s = k[n_p:].reshape(db, t_new, n_heads, HEAD_DIM)
    v_s = v[n_p:].reshape(db, t_new, n_heads, HEAD_DIM)
    return y_p, y_s, k_p, v_p, lf_p, hist_p, k_s, v_s, lf_s, hist_s


def kernel(x_prompt, x_sample, cache_k, cache_v, cache_logf, state_pool, page_table, norm_mix_g, w_in,
           b_forget, q_norm_g, k_norm_g, pool_mix_w, pool_scale, w_branch_pool, w_branch_attn, w_out,
           norm_ffn_g, w_query, sub_keys, peer_u, peer_v):
    depth = w_in.shape[0]
    x_p, x_s = x_prompt, x_sample
    outs = [[] for _ in range(8)]
    for l in range(depth):
        res = _layer(x_p, x_s, cache_k[l], cache_v[l], cache_logf[l], state_pool[l], page_table,
                     norm_mix_g[l], w_in[l], b_forget[l], q_norm_g[l], k_norm_g[l], pool_mix_w[l],
                     pool_scale[l], w_branch_pool[l], w_branch_attn[l], w_out[l], norm_ffn_g[l],
                     w_query[l], sub_keys[l], peer_u[l], peer_v[l])
        x_p, x_s = res[0], res[1]
        for acc, r in zip(outs, res[2:]):
            acc.append(r)
    return (x_p, x_s) + tuple(jnp.stack(o) for o in outs)
```

```python
import functools

import numpy as np
import jax
import jax.numpy as jnp
from jax import lax
from jax.experimental import pallas as pl
from jax.experimental.pallas import tpu as pltpu

F32 = jnp.float32
BF16 = jnp.bfloat16

LANES = 128
SUBLANES = 8
VMEM_LIMIT = 56 * 1024 * 1024

HEAD_DIM = 64
POOL_WINDOWS = (2, 4, 8, 16)
POOL_HIST = 15
PAGE = 128
N_KEYS = 128
PEER_TOPK = 16
RMS_EPS = 1e-6
ATTN_SCALE = HEAD_DIM ** -0.5
HIGHEST = lax.Precision.HIGHEST


def _params(sem, vmem=VMEM_LIMIT):
    return pltpu.CompilerParams(dimension_semantics=sem, vmem_limit_bytes=vmem)


def _full(shape):
    n = len(shape)
    return pl.BlockSpec(shape, lambda *_: (0,) * n)


def _log_sigmoid(z):
    return jnp.minimum(z, 0.0) - jnp.log1p(jnp.exp(-jnp.abs(z)))


def _group_mean_sq(x, gmat):
    xx = x * x
    hi = xx.astype(BF16)
    lo = (xx - hi.astype(F32)).astype(BF16)
    s = jnp.dot(hi, gmat, preferred_element_type=F32) + jnp.dot(lo, gmat, preferred_element_type=F32)
    return s * (1.0 / HEAD_DIM)


def _proj_kernel(x_ref, g_ref, w_ref, bf_ref, qg_ref, kg_ref, gm_ref,
                 u_ref, qb_ref, k_ref, kb_ref, v_ref, vb_ref, lf_ref):
    x = x_ref[...]
    ms = jnp.mean(x * x, axis=-1, keepdims=True)
    xn = (x * lax.rsqrt(ms + RMS_EPS) * g_ref[...]).astype(BF16)
    p = jnp.dot(xn, w_ref[...], preferred_element_type=F32)
    d = u_ref.shape[-1]
    u = p[:, 0:d]
    q = p[:, d:2 * d]
    k = p[:, 2 * d:3 * d]
    v = p[:, 3 * d:4 * d]
    f = p[:, 4 * d:4 * d + LANES]
    gm = gm_ref[...]
    qn = q * lax.rsqrt(_group_mean_sq(q, gm) + RMS_EPS) * qg_ref[...]
    kn = k * lax.rsqrt(_group_mean_sq(k, gm) + RMS_EPS) * kg_ref[...]
    u_ref[...] = u
    qb_ref[...] = (qn * ATTN_SCALE).astype(BF16)
    k_ref[...] = kn
    kb_ref[...] = kn.astype(BF16)
    v_ref[...] = v
    vb_ref[...] = v.astype(BF16)
    lf = _log_sigmoid(f + bf_ref[...])
    lf_ref[...] = lf[:, 0:lf_ref.shape[-1]]


def _project(x, g_mix, w_a, b_f, qg, kg, gmat, n_heads, tm):
    n, dm = x.shape
    d = n_heads * HEAD_DIM
    row = lambda w: pl.BlockSpec((tm, w), lambda i: (i, 0))
    outs = (
        jax.ShapeDtypeStruct((n, d), F32),
        jax.ShapeDtypeStruct((n, d), BF16),
        jax.ShapeDtypeStruct((n, d), F32),
        jax.ShapeDtypeStruct((n, d), BF16),
        jax.ShapeDtypeStruct((n, d), F32),
        jax.ShapeDtypeStruct((n, d), BF16),
        jax.ShapeDtypeStruct((n, n_heads), F32),
    )
    return pl.pallas_call(
        _proj_kernel,
        grid=(n // tm,),
        in_specs=[row(dm), _full(g_mix.shape), _full(w_a.shape), _full(b_f.shape),
                  _full(qg.shape), _full(kg.shape), _full(gmat.shape)],
        out_specs=(row(d), row(d), row(d), row(d), row(d), row(d), row(n_heads)),
        out_shape=outs,
        compiler_params=_params(("arbitrary",)),
        name="proj_in",
    )(x, g_mix, w_a, b_f, qg, kg, gmat)


def _pool_groups(u_g, win_sum, cnt, w_ref, sc_ref, g):
    m = win_sum / cnt - u_g
    o = jnp.dot(m.astype(BF16), w_ref[g], preferred_element_type=F32)
    return o * sc_ref[:, g * LANES:(g + 1) * LANES]


def _pool_prompt_kernel(u_ref, w_ref, sc_ref, o_ref, hist_ref, ext_ref, *, tm):
    si = pl.program_id(1)
    hp = POOL_HIST + 1

    @pl.when(si == 0)
    def _():
        ext_ref[0:hp, :] = jnp.zeros((hp, ext_ref.shape[1]), F32)

    u = u_ref[0]
    ext_ref[hp:hp + tm, :] = u
    pos = si * tm + lax.broadcasted_iota(jnp.int32, (tm, 1), 0)
    for g, w in enumerate(POOL_WINDOWS):
        ls = slice(g * LANES, (g + 1) * LANES)
        u_g = u[:, ls]
        acc = u_g
        for dlt in range(1, w):
            acc = acc + ext_ref[hp - dlt:hp - dlt + tm, ls]
        cnt = jnp.minimum(pos + 1, w).astype(F32)
        o_ref[0, :, ls] = _pool_groups(u_g, acc, cnt, w_ref, sc_ref, g).astype(o_ref.dtype)
    tail = ext_ref[tm:tm + hp, :]
    hist_ref[0] = tail
    ext_ref[0:hp, :] = tail


def _pool_prompt(u3, w_mix, scale, tm):
    b, s, d = u3.shape
    hp = POOL_HIST + 1
    return pl.pallas_call(
        functools.partial(_pool_prompt_kernel, tm=tm),
        grid=(b, s // tm),
        in_specs=[pl.BlockSpec((1, tm, d), lambda bi, si: (bi, si, 0)),
                  _full(w_mix.shape), _full(scale.shape)],
        out_specs=(pl.BlockSpec((1, tm, d), lambda bi, si: (bi, si, 0)),
                   pl.BlockSpec((1, hp, d), lambda bi, si: (bi, 0, 0))),
        out_shape=(jax.ShapeDtypeStruct((b, s, d), BF16),
                   jax.ShapeDtypeStruct((b, hp, d), F32)),
        scratch_shapes=[pltpu.VMEM((hp + tm, d), F32)],
        compiler_params=_params(("arbitrary", "arbitrary")),
        name="pool_prompt",
    )(u3, w_mix, scale)


def _pool_sample_kernel(u_ref, st_ref, w_ref, sc_ref, o_ref, ext_ref, *, pos0):
    nb, t, d = u_ref.shape
    hp = POOL_HIST + 1
    ext_ref[:, 0:hp, :] = st_ref[...]
    ext_ref[:, hp:hp + t, :] = u_ref[...]
    pos = pos0 + lax.broadcasted_iota(jnp.int32, (1, t, 1), 1)
    for g, w in enumerate(POOL_WINDOWS):
        ls = slice(g * LANES, (g + 1) * LANES)
        u_g = u_ref[:, :, ls]
        acc = u_g
        for dlt in range(1, w):
            acc = acc + ext_ref[:, hp - dlt:hp - dlt + t, ls]
        cnt = jnp.minimum(pos + 1, w).astype(F32)
        m = (acc / cnt - u_g).reshape(nb * t, LANES)
        o = jnp.dot(m.astype(BF16), w_ref[g], preferred_element_type=F32) * sc_ref[:, ls]
        o_ref[:, ls] = o.astype(o_ref.dtype)


def _pool_sample(u3, state16, w_mix, scale, pos0):
    nb, t, d = u3.shape
    hp = POOL_HIST + 1
    return pl.pallas_call(
        functools.partial(_pool_sample_kernel, pos0=pos0),
        grid=(1,),
        in_specs=[_full(u3.shape), _full(state16.shape), _full(w_mix.shape), _full(scale.shape)],
        out_specs=_full((nb * t, d)),
        out_shape=jax.ShapeDtypeStruct((nb * t, d), BF16),
        scratch_shapes=[pltpu.VMEM((nb, hp + t, d), F32)],
        compiler_params=_params(("arbitrary",)),
        name="pool_sample",
    )(u3, state16, w_mix, scale)


def _upper_tri(n):
    r = lax.broadcasted_iota(jnp.int32, (n, n), 0)
    c = lax.broadcasted_iota(jnp.int32, (n, n), 1)
    return (r <= c).astype(F32)


def _cumsum_prompt_kernel(lf_ref, c_ref, carry_ref):
    @pl.when(pl.program_id(1) == 0)
    def _():
        carry_ref[...] = jnp.zeros_like(carry_ref)

    tc = lf_ref.shape[-1]
    c = carry_ref[...] + jnp.dot(lf_ref[0], _upper_tri(tc), precision=HIGHEST, preferred_element_type=F32)
    c_ref[0] = c
    carry_ref[...] = jnp.broadcast_to(c[:, tc - 1:tc], c.shape)


def _cumsum_prompt(lf_t, tc):
    b, h, s = lf_t.shape
    return pl.pallas_call(
        _cumsum_prompt_kernel,
        grid=(b, s // tc),
        in_specs=[pl.BlockSpec((1, h, tc), lambda bi, si: (bi, 0, si))],
        out_specs=pl.BlockSpec((1, h, tc), lambda bi, si: (bi, 0, si)),
        out_shape=jax.ShapeDtypeStruct((b, h, s), F32),
        scratch_shapes=[pltpu.VMEM((h, tc), F32)],
        compiler_params=_params(("arbitrary", "arbitrary")),
        name="cumsum_prompt",
    )(lf_t)


def _cumsum_sample_kernel(pt_ref, lfp_ref, lfn_ref, c_ref, carry_ref, *, n_pages):
    j = pl.program_id(1)

    @pl.when(j == 0)
    def _():
        carry_ref[...] = jnp.zeros_like(carry_ref)

    tri = _upper_tri(PAGE)

    @pl.when(j < n_pages)
    def _():
        c = carry_ref[...] + jnp.dot(lfp_ref[0], tri, precision=HIGHEST, preferred_element_type=F32)
        c_ref[0] = c
        carry_ref[...] = jnp.broadcast_to(c[:, PAGE - 1:PAGE], c.shape)

    @pl.when(j == n_pages)
    def _():
        c_ref[0] = carry_ref[...] + jnp.dot(lfn_ref[0], tri, precision=HIGHEST, preferred_element_type=F32)


def _cumsum_sample(page_table, lf_cache_t, lf_new_t):
    db, n_pages = page_table.shape
    h = lf_cache_t.shape[1]
    grid_spec = pltpu.PrefetchScalarGridSpec(
        num_scalar_prefetch=1,
        grid=(db, n_pages + 1),
        in_specs=[pl.BlockSpec((1, h, PAGE), lambda b, j, pt: (pt[b, jnp.minimum(j, n_pages - 1)], 0, 0)),
                  pl.BlockSpec((1, h, PAGE), lambda b, j, pt: (b, 0, 0))],
        out_specs=pl.BlockSpec((1, h, PAGE), lambda b, j, pt: (b, 0, j)),
        scratch_shapes=[pltpu.VMEM((h, PAGE), F32)],
    )
    return pl.pallas_call(
        functools.partial(_cumsum_sample_kernel, n_pages=n_pages),
        grid_spec=grid_spec,
        out_shape=jax.ShapeDtypeStruct((db, h, (n_pages + 1) * PAGE), F32),
        compiler_params=_params(("arbitrary", "arbitrary")),
        name="cumsum_sample",
    )(page_table, lf_cache_t, lf_new_t)


def _attn_prompt_kernel(q_ref, k_ref, v_ref, cq_ref, ck_ref, o_ref, m_ref, l_ref, acc_ref, *, tq, tk):
    qi = pl.program_id(2)
    ki = pl.program_id(3)
    nk = pl.num_programs(3)

    @pl.when(ki == 0)
    def _():
        m_ref[...] = jnp.full_like(m_ref, -jnp.inf)
        l_ref[...] = jnp.zeros_like(l_ref)
        acc_ref[...] = jnp.zeros_like(acc_ref)

    @pl.when(ki <= qi)
    def _():
        q = q_ref[0]
        k = k_ref[0]
        v = v_ref[0]
        lane = lax.broadcasted_iota(jnp.int32, (1, LANES), 1)
        qpos = qi * tq + lax.broadcasted_iota(jnp.int32, (tq, tk), 0)
        kpos = ki * tk + lax.broadcasted_iota(jnp.int32, (tq, tk), 1)
        visible = kpos <= qpos
        for hh in range(2):
            in_head = (lane >= hh * HEAD_DIM) & (lane < (hh + 1) * HEAD_DIM)
            qm = jnp.where(in_head, q, jnp.zeros_like(q))
            s = lax.dot_general(qm, k, (((1,), (1,)), ((), ())), preferred_element_type=F32)
            s = s + (cq_ref[0, 0, :, hh:hh + 1] - ck_ref[0, 0, hh:hh + 1, :])
            s = jnp.where(visible, s, -jnp.inf)
            m_prev = m_ref[hh]
            m_new = jnp.maximum(m_prev, jnp.max(s, axis=1, keepdims=True))
            alpha = jnp.exp(m_prev - m_new)
            p = jnp.exp(s - m_new)
            l_ref[hh] = alpha * l_ref[hh] + jnp.sum(p, axis=1, keepdims=True)
            acc_ref[hh] = alpha * acc_ref[hh] + jnp.dot(p.astype(BF16), v, preferred_element_type=F32)
            m_ref[hh] = m_new

    @pl.when(ki == nk - 1)
    def _():
        lane = lax.broadcasted_iota(jnp.int32, (1, LANES), 1)
        o0 = acc_ref[0] / l_ref[0]
        o1 = acc_ref[1] / l_ref[1]
        o_ref[0] = jnp.where(lane < HEAD_DIM, o0, o1).astype(o_ref.dtype)


def _attn_prompt(qb, kb, vb, cq, ck, tq, tk):
    b, s, d = qb.shape
    npair = d // LANES
    return pl.pallas_call(
        functools.partial(_attn_prompt_kernel, tq=tq, tk=tk),
        grid=(b, npair, s // tq, s // tk),
        in_specs=[
            pl.BlockSpec((1, tq, LANES), lambda bi, p, qi, ki: (bi, qi, p)),
            pl.BlockSpec((1, tk, LANES), lambda bi, p, qi, ki: (bi, jnp.minimum(ki, qi), p)),
            pl.BlockSpec((1, tk, LANES), lambda bi, p, qi, ki: (bi, jnp.minimum(ki, qi), p)),
            pl.BlockSpec((1, 1, tq, 2), lambda bi, p, qi, ki: (bi, p, qi, 0)),
            pl.BlockSpec((1, 1, 2, tk), lambda bi, p, qi, ki: (bi, p, 0, jnp.minimum(ki, qi))),
        ],
        out_specs=pl.BlockSpec((1, tq, LANES), lambda bi, p, qi, ki: (bi, qi, p)),
        out_shape=jax.ShapeDtypeStruct((b, s, d), BF16),
        scratch_shapes=[pltpu.VMEM((2, tq, 1), F32), pltpu.VMEM((2, tq, 1), F32),
                        pltpu.VMEM((2, tq, LANES), F32)],
        compiler_params=_params(("arbitrary",) * 4),
        name="attn_prompt",
    )(qb, kb, vb, cq, ck)


def _attn_sample_kernel(pt_ref, q_ref, kc_ref, vc_ref, kn_ref, vn_ref, ck_ref, cn_ref, o_ref,
                        qbd_ref, cq_ref, m_ref, l_ref, acc_ref, *, n_pages, n_heads, t_new):
    j = pl.program_id(1)
    d = n_heads * HEAD_DIM
    rows = t_new * n_heads
    head_of_lane = lax.broadcasted_iota(jnp.int32, (n_heads, d), 1) // HEAD_DIM
    head_mask = head_of_lane == lax.broadcasted_iota(jnp.int32, (n_heads, d), 0)

    @pl.when(j == 0)
    def _():
        q = q_ref[0].astype(F32)
        cn = cn_ref[0]
        for t in range(t_new):
            qt = jnp.broadcast_to(q[t:t + 1, :], (n_heads, d))
            qbd_ref[t * n_heads:(t + 1) * n_heads, :] = jnp.where(head_mask, qt, jnp.zeros_like(qt))
            cq_ref[t * n_heads:(t + 1) * n_heads, :] = jnp.broadcast_to(cn[:, t:t + 1], (n_heads, PAGE))
        m_ref[...] = jnp.full_like(m_ref, -jnp.inf)
        l_ref[...] = jnp.zeros_like(l_ref)
        acc_ref[...] = jnp.zeros_like(acc_ref)

    def step(kb, vb, visible):
        s = lax.dot_general(qbd_ref[...].astype(BF16), kb, (((1,), (1,)), ((), ())),
                            preferred_element_type=F32)
        ck = ck_ref[0]
        s = s + (cq_ref[...] - jnp.concatenate([ck] * t_new, axis=0))
        if visible is not None:
            s = jnp.where(visible, s, -jnp.inf)
        m_prev = m_ref[...]
        m_new = jnp.maximum(m_prev, jnp.max(s, axis=1, keepdims=True))
        alpha = jnp.exp(m_prev - m_new)
        p = jnp.exp(s - m_new)
        l_ref[...] = alpha * l_ref[...] + jnp.sum(p, axis=1, keepdims=True)
        acc_ref[...] = alpha * acc_ref[...] + jnp.dot(p.astype(BF16), vb, preferred_element_type=F32)
        m_ref[...] = m_new

    @pl.when(j < n_pages)
    def _():
        step(kc_ref[0].astype(BF16), vc_ref[0].astype(BF16), None)

    @pl.when(j == n_pages)
    def _():
        pad = jnp.zeros((PAGE - t_new, d), F32)
        kb = jnp.concatenate([kn_ref[0], pad], axis=0).astype(BF16)
        vb = jnp.concatenate([vn_ref[0], pad], axis=0).astype(BF16)
        key = lax.broadcasted_iota(jnp.int32, (rows, PAGE), 1)
        tok = lax.broadcasted_iota(jnp.int32, (rows, PAGE), 0) // n_heads
        step(kb, vb, key <= tok)
        o = acc_ref[...] / l_ref[...]
        out_rows = []
        for t in range(t_new):
            ot = jnp.where(head_mask, o[t * n_heads:(t + 1) * n_heads, :], 0.0)
            out_rows.append(jnp.sum(ot, axis=0, keepdims=True))
        o_ref[0] = jnp.concatenate(out_rows, axis=0)


def _attn_sample(page_table, qb, k_cache, v_cache, k_new, v_new, c_all, n_heads):
    db, t_new, d = qb.shape
    n_pages = page_table.shape[1]
    rows = t_new * n_heads
    page_idx = lambda b, j, pt: (pt[b, jnp.minimum(j, n_pages - 1)], 0, 0)
    per_seq = lambda b, j, pt: (b, 0, 0)
    grid_spec = pltpu.PrefetchScalarGridSpec(
        num_scalar_prefetch=1,
        grid=(db, n_pages + 1),
        in_specs=[
            pl.BlockSpec((1, t_new, d), per_seq),
            pl.BlockSpec((1, PAGE, d), page_idx),
            pl.BlockSpec((1, PAGE, d), page_idx),
            pl.BlockSpec((1, t_new, d), per_seq),
            pl.BlockSpec((1, t_new, d), per_seq),
            pl.BlockSpec((1, n_heads, PAGE), lambda b, j, pt: (b, 0, j)),
            pl.BlockSpec((1, n_heads, PAGE), lambda b, j, pt: (b, 0, n_pages)),
        ],
        out_specs=pl.BlockSpec((1, t_new, d), per_seq),
        scratch_shapes=[pltpu.VMEM((rows, d), F32), pltpu.VMEM((rows, PAGE), F32),
                        pltpu.VMEM((rows, 1), F32), pltpu.VMEM((rows, 1), F32),
                        pltpu.VMEM((rows, d), F32)],
    )
    return pl.pallas_call(
        functools.partial(_attn_sample_kernel, n_pages=n_pages, n_heads=n_heads, t_new=t_new),
        grid_spec=grid_spec,
        out_shape=jax.ShapeDtypeStruct((db, t_new, d), F32),
        compiler_params=_params(("arbitrary", "arbitrary")),
        name="attn_sample",
    )(page_table, qb, k_cache, v_cache, k_new, v_new, c_all, c_all)


def _merge_kernel(x_ref, pool_ref, attn_ref, gmix_ref, wg_ref, wbp_ref, wba_ref, wo_ref, gffn_ref, wq_ref,
                  h_ref, hnt_ref, qp_ref):
    x = x_ref[...]
    dm = x.shape[1]
    ms = jnp.mean(x * x, axis=-1, keepdims=True)
    xn = (x * lax.rsqrt(ms + RMS_EPS) * gmix_ref[...]).astype(BF16)
    gates = jax.nn.sigmoid(jnp.dot(xn, wg_ref[...], preferred_element_type=F32))
    mixed = (gates[:, 0:dm] * jnp.dot(pool_ref[...], wbp_ref[...], preferred_element_type=F32)
             + gates[:, dm:2 * dm] * jnp.dot(attn_ref[...], wba_ref[...], preferred_element_type=F32))
    h = x + jnp.dot(mixed.astype(BF16), wo_ref[...], preferred_element_type=F32)
    h_ref[...] = h
    hms = jnp.mean(h * h, axis=-1, keepdims=True)
    hn = h * lax.rsqrt(hms + RMS_EPS) * gffn_ref[...]
    hnb = hn.astype(BF16)
    hnt_ref[...] = hn.T.astype(BF16)
    qp_ref[...] = jnp.dot(hnb, wq_ref[...], preferred_element_type=F32).astype(BF16)


def _merge(x, pool, attn, g_mix, w_gates, wbp, wba, wo, g_ffn, wq, tm):
    n, dm = x.shape
    dq = wq.shape[1]
    row = lambda w: pl.BlockSpec((tm, w), lambda i: (i, 0))
    return pl.pallas_call(
        _merge_kernel,
        grid=(n // tm,),
        in_specs=[row(dm), row(pool.shape[1]), row(attn.shape[1]), _full(g_mix.shape), _full(w_gates.shape),
                  _full(wbp.shape), _full(wba.shape), _full(wo.shape), _full(g_ffn.shape), _full(wq.shape)],
        out_specs=(row(dm), pl.BlockSpec((dm, tm), lambda i: (0, i)), row(dq)),
        out_shape=(jax.ShapeDtypeStruct((n, dm), F32),
                   jax.ShapeDtypeStruct((dm, n), BF16),
                   jax.ShapeDtypeStruct((n, dq), BF16)),
        compiler_params=_params(("arbitrary",)),
        name="merge",
    )(x, pool, attn, g_mix, w_gates, wbp, wba, wo, g_ffn, wq)


def _oddeven_merge_sort_pairs(n):
    pairs = []
    p = 1
    while p < n:
        k = p
        while k >= 1:
            for j in range(k % p, n - k, 2 * k):
                for i in range(min(k, n - j - k)):
                    if (i + j) // (2 * p) == (i + j + k) // (2 * p):
                        pairs.append((i + j, i + j + k))
            k //= 2
        p *= 2
    return pairs


def _bitonic_merge_pairs(n):
    pairs = []
    k = n // 2
    while k >= 1:
        for i in range(n):
            if (i // k) % 2 == 0 and i + k < n:
                pairs.append((i, i + k))
        k //= 2
    return pairs


_SORT16 = _oddeven_merge_sort_pairs(16)
_MERGE16 = _bitonic_merge_pairs(16)


def _apply_desc(vals, pairs):
    vals = list(vals)
    for a, b in pairs:
        hi = jnp.maximum(vals[a], vals[b])
        lo = jnp.minimum(vals[a], vals[b])
        vals[a], vals[b] = hi, lo
    return vals


def _top16_across_sublanes(vals):
    for shift in (4, 2, 1):
        other = [pltpu.roll(v, shift, axis=0) for v in vals]
        vals = [jnp.maximum(vals[i], other[PEER_TOPK - 1 - i]) for i in range(PEER_TOPK)]
        vals = _apply_desc(vals, _MERGE16)
    return vals


def _topk_tile(s_tile):
    vals = [s_tile[SUBLANES * i:SUBLANES * (i + 1), :] for i in range(N_KEYS // SUBLANES)]
    vals = _apply_desc(vals, _SORT16)
    return _top16_across_sublanes(vals)


def _route_kernel(qp_ref, sk_ref, s1_ref, s2_ref, st_ref, *, n_heads, lt):
    tn = qp_ref.shape[0]
    neg = -jnp.inf
    sub = lax.broadcasted_iota(jnp.int32, (SUBLANES, lt), 0)
    for h in range(n_heads):
        for half, dst in ((0, s1_ref), (1, s2_ref)):
            c0 = (2 * h + half) * LANES
            dst[h] = lax.dot_general(sk_ref[2 * h + half], qp_ref[:, c0:c0 + LANES],
                                     (((1,), (1,)), ((), ())), preferred_element_type=F32)

    def body(it, carry):
        h = it // (tn // lt)
        l0 = pl.multiple_of((it % (tn // lt)) * lt, lt)
        a = _topk_tile(s1_ref[h, :, pl.ds(l0, lt)])
        b = _topk_tile(s2_ref[h, :, pl.ds(l0, lt)])
        def pack(vs):
            out = vs[0]
            for i in range(1, len(vs)):
                out = jnp.where(sub == i, vs[i], out)
            return out
        b_lo, b_hi, a_hi = pack(b[0:8]), pack(b[8:16]), pack(a[8:16])
        cands = [a[0] + b_lo, a[0] + b_hi, a[1] + b_lo]
        for k in range(2, 8):
            cands.append(jnp.where(sub < PEER_TOPK // (k + 1), a[k] + b_lo, neg))
        cands.append(a_hi + b[0])
        cands = cands + [jnp.full((SUBLANES, lt), neg, F32)] * (PEER_TOPK - len(cands))
        top = _top16_across_sublanes(_apply_desc(cands, _SORT16))
        z = jnp.zeros((SUBLANES, lt), F32)
        for i in range(PEER_TOPK):
            z = z + jnp.exp(top[i] - top[0])
        row = lambda v: v[0:1, :]
        st_ref[0, h, :, pl.ds(l0, lt)] = row(top[PEER_TOPK - 1])
        st_ref[1, h, :, pl.ds(l0, lt)] = row(a[0])
        st_ref[2, h, :, pl.ds(l0, lt)] = row(b[0])
        st_ref[3, h, :, pl.ds(l0, lt)] = row(1.0 / z)
        return carry

    lax.fori_loop(0, n_heads * (tn // lt), body, 0)


def _route(qp, sub_keys_b, n_heads, tn, lt=LANES):
    n, dq = qp.shape
    return pl.pallas_call(
        functools.partial(_route_kernel, n_heads=n_heads, lt=lt),
        grid=(n // tn,),
        in_specs=[pl.BlockSpec((tn, dq), lambda i: (i, 0)), _full(sub_keys_b.shape)],
        out_specs=(pl.BlockSpec((n_heads, N_KEYS, tn), lambda i: (0, 0, i)),
                   pl.BlockSpec((n_heads, N_KEYS, tn), lambda i: (0, 0, i)),
                   pl.BlockSpec((4, n_heads, 1, tn), lambda i: (0, 0, 0, i))),
        out_shape=(jax.ShapeDtypeStruct((n_heads, N_KEYS, n), F32),
                   jax.ShapeDtypeStruct((n_heads, N_KEYS, n), F32),
                   jax.ShapeDtypeStruct((4, n_heads, 1, n), F32)),
        compiler_params=_params(("arbitrary",)),
        name="peer_route",
    )(qp, sub_keys_b)


JV_UNROLL = 4
JV_SHIFT = (N_KEYS // SUBLANES // JV_UNROLL).bit_length() - 1


def _tree_sum(terms):
    while len(terms) > 1:
        terms = [terms[i] + terms[i + 1] for i in range(0, len(terms) - 1, 2)] + \
                ([terms[-1]] if len(terms) % 2 else [])
    return terms[0]


def _peer_kernel(hnt_ref, s1_ref, s2_ref, st_ref, u_ref, vt_ref, h_ref, y_ref,
                 p1_ref, se_ref, thr_ref, sp_ref, a_ref, wg_ref, acc_ref, *, n_heads, rows_per_step):
    e = pl.program_id(1)
    ne = pl.num_programs(1)
    tt = hnt_ref.shape[1]
    n_lt = tt // LANES
    n_jv = N_KEYS // SUBLANES
    lane_tile = lambda tl: slice(tl * LANES, (tl + 1) * LANES)

    @pl.when(e == 0)
    def _():
        acc_ref[...] = jnp.zeros_like(acc_ref)
        for h in range(n_heads):
            p1_ref[h] = jnp.exp(s1_ref[h] - st_ref[1, h]) * st_ref[3, h]
            s2h = s2_ref[h]
            e2h = jnp.exp(s2h - st_ref[2, h])
            for tl in range(n_lt):
                se_ref[tl, :, h, 0] = s2h[:, lane_tile(tl)].reshape(n_jv, SUBLANES, LANES)
                se_ref[tl, :, h, 1] = e2h[:, lane_tile(tl)].reshape(n_jv, SUBLANES, LANES)
                thr_ref[tl, h] = jnp.broadcast_to(st_ref[0, h, :, lane_tile(tl)], (SUBLANES, LANES))

    a_ref[...] = jnp.dot(u_ref[...], hnt_ref[...], preferred_element_type=F32)

    i0 = pl.multiple_of(e * rows_per_step, SUBLANES)
    for h in range(n_heads):
        s1_rows = s1_ref[h, pl.ds(i0, rows_per_step), :]
        p1_rows = p1_ref[h, pl.ds(i0, rows_per_step), :]
        for r in range(rows_per_step):
            for tl in range(n_lt):
                sp_ref[tl, h, r, 0] = jnp.broadcast_to(s1_rows[r:r + 1, lane_tile(tl)], (SUBLANES, LANES))
                sp_ref[tl, h, r, 1] = jnp.broadcast_to(p1_rows[r:r + 1, lane_tile(tl)], (SUBLANES, LANES))

    def key_tile(tl, jv, lanes):
        j0 = pl.multiple_of(jv * SUBLANES, SUBLANES)
        s2 = [se_ref[tl, jv, h, 0] for h in range(n_heads)]
        e2 = [se_ref[tl, jv, h, 1] for h in range(n_heads)]
        thr = [thr_ref[tl, h] for h in range(n_heads)]
        for r in range(rows_per_step):
            terms = [jnp.where(s2[h] + sp_ref[tl, h, r, 0] >= thr[h], e2[h] * sp_ref[tl, h, r, 1], 0.0)
                     for h in range(n_heads)]
            r0 = pl.multiple_of(r * N_KEYS + j0, SUBLANES)
            act = a_ref[pl.ds(r0, SUBLANES), lanes]
            gelu = 0.5 * act * (1.0 + lax.erf(act * np.float32(1.0 / np.sqrt(2.0))))
            wg_ref[pl.ds(r0, SUBLANES), lanes] = _tree_sum(terms) * gelu

    def body(it, carry):
        tl = lax.shift_right_logical(it, JV_SHIFT)
        jg = lax.bitwise_and(it, n_jv // JV_UNROLL - 1)
        lanes = pl.ds(pl.multiple_of(tl * LANES, LANES), LANES)
        for unroll in range(JV_UNROLL):
            key_tile(tl, jg * JV_UNROLL + unroll, lanes)
        return carry

    lax.fori_loop(0, n_lt * n_jv // JV_UNROLL, body, 0)
    acc_ref[...] += jnp.dot(vt_ref[...], wg_ref[...].astype(BF16), preferred_element_type=F32)

    @pl.when(e == ne - 1)
    def _():
        y_ref[...] = h_ref[...] + acc_ref[...].T


def _peer(hnt, s1, s2, stats, u_b, vt_b, h, tt, rows_per_step):
    dm, n = hnt.shape
    n_heads = s1.shape[0]
    n_exp = u_b.shape[0]
    eb = rows_per_step * N_KEYS
    return pl.pallas_call(
        functools.partial(_peer_kernel, n_heads=n_heads, rows_per_step=rows_per_step),
        grid=(n // tt, n_exp // eb),
        in_specs=[
            pl.BlockSpec((dm, tt), lambda t, e: (0, t)),
            pl.BlockSpec((n_heads, N_KEYS, tt), lambda t, e: (0, 0, t)),
            pl.BlockSpec((n_heads, N_KEYS, tt), lambda t, e: (0, 0, t)),
            pl.BlockSpec((4, n_heads, 1, tt), lambda t, e: (0, 0, 0, t)),
            pl.BlockSpec((eb, dm), lambda t, e: (e, 0)),
            pl.BlockSpec((dm, eb), lambda t, e: (0, e)),
            pl.BlockSpec((tt, dm), lambda t, e: (t, 0)),
        ],
        out_specs=pl.BlockSpec((tt, dm), lambda t, e: (t, 0)),
        out_shape=jax.ShapeDtypeStruct((n, dm), F32),
        scratch_shapes=[pltpu.VMEM((n_heads, N_KEYS, tt), F32),
                        pltpu.VMEM((tt // LANES, N_KEYS // SUBLANES, n_heads, 2, SUBLANES, LANES), F32),
                        pltpu.VMEM((tt // LANES, n_heads, SUBLANES, LANES), F32),
                        pltpu.VMEM((tt // LANES, n_heads, rows_per_step, 2, SUBLANES, LANES), F32),
                        pltpu.VMEM((eb, tt), F32), pltpu.VMEM((eb, tt), F32), pltpu.VMEM((dm, tt), F32)],
        compiler_params=_params(("arbitrary", "arbitrary")),
        name="peer_dense",
    )(hnt, s1, s2, stats, u_b, vt_b, h)


def _layer(x_p, x_s, cache_k, cache_v, cache_logf, state_pool, page_table, norm_mix_g, w_in, b_forget,
           q_norm_g, k_norm_g, pool_mix_w, pool_scale, w_branch_pool, w_branch_attn, w_out, norm_ffn_g,
           w_query, sub_keys, peer_u, peer_v):
    b, s, dm = x_p.shape
    db, t_new, _ = x_s.shape
    n_heads = b_forget.shape[0]
    d = n_heads * HEAD_DIM
    n_p, n_s = b * s, db * t_new
    n = n_p + n_s
    past_len = page_table.shape[1] * PAGE
    tm = 512

    w_a = jnp.concatenate([w_in[:, 0:4 * d + n_heads],
                           jnp.zeros((dm, LANES - n_heads), w_in.dtype)], axis=1).astype(BF16)
    w_gates = w_in[:, 4 * d + n_heads:].astype(BF16)
    b_f = jnp.concatenate([b_forget, jnp.zeros((LANES - n_heads,), F32)]).reshape(1, LANES)
    qg = jnp.tile(q_norm_g, n_heads).reshape(1, d)
    kg = jnp.tile(k_norm_g, n_heads).reshape(1, d)
    lane_head = np.arange(d) // HEAD_DIM
    gmat = jnp.asarray(lane_head[:, None] == lane_head[None, :], BF16)
    g_mix = norm_mix_g.reshape(1, dm)
    g_ffn = norm_ffn_g.reshape(1, dm)
    w_mix = pool_mix_w.astype(BF16)
    p_scale = pool_scale.reshape(1, d)

    x_all = jnp.concatenate([x_p.reshape(n_p, dm), x_s.reshape(n_s, dm)], axis=0)
    u, qb, k, kb, v, vb, lf = _project(x_all, g_mix, w_a, b_f, qg, kg, gmat, n_heads, tm)

    pool_p, hist16 = _pool_prompt(u[:n_p].reshape(b, s, d), w_mix, p_scale, tm)
    state16 = jnp.concatenate([jnp.zeros((db, 1, d), F32), state_pool], axis=1)
    u_s = u[n_p:].reshape(db, t_new, d)
    pool_s = _pool_sample(u_s, state16, w_mix, p_scale, past_len)
    hist_p = hist16[:, 1:, :]
    hist_s = jnp.concatenate([state_pool, u_s], axis=1)[:, -POOL_HIST:, :]

    lf_p = lf[:n_p].reshape(b, s, n_heads)
    c_t = _cumsum_prompt(jnp.transpose(lf_p, (0, 2, 1)), 512)
    ck = c_t.reshape(b, n_heads // 2, 2, s)
    cq = jnp.transpose(ck, (0, 1, 3, 2))
    attn_p = _attn_prompt(qb[:n_p].reshape(b, s, d), kb[:n_p].reshape(b, s, d), vb[:n_p].reshape(b, s, d),
                          cq, ck, 512, 512)

    lf_s = lf[n_p:].reshape(db, t_new, n_heads)
    lf_new_t = jnp.concatenate([jnp.transpose(lf_s, (0, 2, 1)),
                                jnp.zeros((db, n_heads, PAGE - t_new), F32)], axis=2)
    lf_cache_t = jnp.transpose(cache_logf, (0, 2, 1))
    c_all = _cumsum_sample(page_table, lf_cache_t, lf_new_t)
    n_phys = cache_k.shape[0]
    attn_s = _attn_sample(page_table, qb[n_p:].reshape(db, t_new, d),
                          cache_k.reshape(n_phys, PAGE, d), cache_v.reshape(n_phys, PAGE, d),
                          k[n_p:].reshape(db, t_new, d), v[n_p:].reshape(db, t_new, d), c_all, n_heads)

    pool_all = jnp.concatenate([pool_p.reshape(n_p, d), pool_s], axis=0)
    attn_all = jnp.concatenate([attn_p.reshape(n_p, d), attn_s.reshape(n_s, d).astype(BF16)], axis=0)
    h, hnt, qp = _merge(x_all, pool_all, attn_all, g_mix, w_gates, w_branch_pool.astype(BF16),
                        w_branch_attn.astype(BF16), w_out.astype(BF16), g_ffn, w_query.astype(BF16), tm)
    n_ph = sub_keys.shape[0]
    sk_b = sub_keys.reshape(n_ph * 2, N_KEYS, sub_keys.shape[-1]).astype(BF16)
    s1, s2, stats = _route(qp, sk_b, n_ph, 512)
    y = _peer(hnt, s1, s2, stats, peer_u.astype(BF16), jnp.transpose(peer_v).astype(BF16), h, 512, 8)

    y_p = y[:n_p].reshape(b, s, dm)
    y_s = y[n_p:].reshape(db, t_new, dm)
    k_p = k[:n_p].reshape(b, s, n_heads, HEAD_DIM)
    v_p = v[:n_p].reshape(b, s, n_heads, HEAD_DIM)
    k_s = k[n_p:].reshape(db, t_new, n_heads, HEAD_DIM)
    v_s = v[n_p:].reshape(db, t_new, n_heads, HEAD_DIM)
    return y_p, y_s, k_p, v_p, lf_p, hist_p, k_s, v_s, lf_s, hist_s


def kernel(x_prompt, x_sample, cache_k, cache_v, cache_logf, state_pool, page_table, norm_mix_g, w_in,
           b_forget, q_norm_g, k_norm_g, pool_mix_w, pool_scale, w_branch_pool, w_branch_attn, w_out,
           norm_ffn_g, w_query, sub_keys, peer_u, peer_v):
    depth = w_in.shape[0]
    x_p, x_s = x_prompt, x_sample
    outs = [[] for _ in range(8)]
    for l in range(depth):
        res = _layer(x_p, x_s, cache_k[l], cache_v[l], cache_logf[l], state_pool[l], page_table,
                     norm_mix_g[l], w_in[l], b_forget[l], q_norm_g[l], k_norm_g[l], pool_mix_w[l],
                     pool_scale[l], w_branch_pool[l], w_branch_attn[l], w_out[l], norm_ffn_g[l],
                     w_query[l], sub_keys[l], peer_u[l], peer_v[l])
        x_p, x_s = res[0], res[1]
        for acc, r in zip(outs, res[2:]):
            acc.append(r)
    return (x_p, x_s) + tuple(jnp.stack(o) for o in outs)
```

```python
import functools

import numpy as np
import jax
import jax.numpy as jnp
from jax import lax
from jax.experimental import pallas as pl
from jax.experimental.pallas import tpu as pltpu

F32 = jnp.float32
BF16 = jnp.bfloat16

LANES = 128
SUBLANES = 8
VMEM_LIMIT = 56 * 1024 * 1024

HEAD_DIM = 64
POOL_WINDOWS = (2, 4, 8, 16)
POOL_HIST = 15
PAGE = 128
N_KEYS = 128
PEER_TOPK = 16
RMS_EPS = 1e-6
ATTN_SCALE = HEAD_DIM ** -0.5
HIGHEST = lax.Precision.HIGHEST

ROW_TILE = 512
CUMSUM_TILE = 512
ATTN_TILE = 1024
PAGES_PER_STEP = 16
PAGE_SUM_ROWS = 512
ROUTE_TILE = 512
PEER_TOKENS = 512
PEER_ROWS = 8


def _params(sem, vmem=VMEM_LIMIT):
    return pltpu.CompilerParams(dimension_semantics=sem, vmem_limit_bytes=vmem)


def _full(shape):
    n = len(shape)
    return pl.BlockSpec(shape, lambda *_: (0,) * n)


def _log_sigmoid(z):
    return jnp.minimum(z, 0.0) - jnp.log1p(jnp.exp(-jnp.abs(z)))


def _group_mean_sq(x, gmat):
    xx = x * x
    hi = xx.astype(BF16)
    lo = (xx - hi.astype(F32)).astype(BF16)
    s = jnp.dot(hi, gmat, preferred_element_type=F32) + jnp.dot(lo, gmat, preferred_element_type=F32)
    return s * (1.0 / HEAD_DIM)


def _proj_kernel(x_ref, g_ref, w_ref, bf_ref, qg_ref, kg_ref, gm_ref,
                 u_ref, qb_ref, k_ref, kb_ref, v_ref, vb_ref, lf_ref, *, kv_transposed):
    x = x_ref[...]
    ms = jnp.mean(x * x, axis=-1, keepdims=True)
    xn = (x * lax.rsqrt(ms + RMS_EPS) * g_ref[...]).astype(BF16)
    p = jnp.dot(xn, w_ref[...], preferred_element_type=F32)
    d = u_ref.shape[-1]
    u = p[:, 0:d]
    q = p[:, d:2 * d]
    k = p[:, 2 * d:3 * d]
    v = p[:, 3 * d:4 * d]
    f = p[:, 4 * d:4 * d + LANES]
    gm = gm_ref[...]
    qn = q * lax.rsqrt(_group_mean_sq(q, gm) + RMS_EPS) * qg_ref[...]
    kn = k * lax.rsqrt(_group_mean_sq(k, gm) + RMS_EPS) * kg_ref[...]
    u_ref[...] = u
    qb_ref[...] = (qn * ATTN_SCALE).astype(BF16)
    kb_ref[...] = kn.astype(BF16)
    if kv_transposed:
        k_ref[0] = kn.T
        v_t = v.T
        v_ref[0] = v_t
        vb_ref[0] = v_t.astype(BF16)
    else:
        k_ref[...] = kn
        v_ref[...] = v
        vb_ref[...] = v.astype(BF16)
    lf = _log_sigmoid(f + bf_ref[...])
    lf_ref[...] = lf[:, 0:lf_ref.shape[-1]]


def _project(x, g_mix, w_a, b_f, qg, kg, gmat, n_heads, tm, seq_len=None):
    n, dm = x.shape
    d = n_heads * HEAD_DIM
    row = lambda w: pl.BlockSpec((tm, w), lambda i: (i, 0))
    rows = lambda dt: jax.ShapeDtypeStruct((n, d), dt)
    kv_spec, kv = row(d), rows
    if seq_len is not None:
        tiles = seq_len // tm
        kv_spec = pl.BlockSpec((1, d, tm), lambda i: (i // tiles, 0, i % tiles))
        kv = lambda dt: jax.ShapeDtypeStruct((n // seq_len, d, seq_len), dt)
    return pl.pallas_call(
        functools.partial(_proj_kernel, kv_transposed=seq_len is not None),
        grid=(n // tm,),
        in_specs=[row(dm), _full(g_mix.shape), _full(w_a.shape), _full(b_f.shape),
                  _full(qg.shape), _full(kg.shape), _full(gmat.shape)],
        out_specs=(row(d), row(d), kv_spec, row(d), kv_spec, kv_spec, row(n_heads)),
        out_shape=(rows(F32), rows(BF16), kv(F32), rows(BF16), kv(F32), kv(BF16),
                   jax.ShapeDtypeStruct((n, n_heads), F32)),
        compiler_params=_params(("arbitrary",)),
        name="proj_in",
    )(x, g_mix, w_a, b_f, qg, kg, gmat)


def _pool_groups(u_g, win_sum, cnt, w_ref, sc_ref, g):
    m = win_sum / cnt - u_g
    o = jnp.dot(m.astype(BF16), w_ref[g], preferred_element_type=F32)
    return o * sc_ref[:, g * LANES:(g + 1) * LANES]


def _pool_prompt_kernel(u_ref, w_ref, sc_ref, o_ref, hist_ref, ext_ref, *, tm):
    si = pl.program_id(1)
    hp = POOL_HIST + 1

    @pl.when(si == 0)
    def _():
        ext_ref[0:hp, :] = jnp.zeros((hp, ext_ref.shape[1]), F32)

    u = u_ref[0]
    ext_ref[hp:hp + tm, :] = u
    pos = si * tm + lax.broadcasted_iota(jnp.int32, (tm, 1), 0)
    for g, w in enumerate(POOL_WINDOWS):
        ls = slice(g * LANES, (g + 1) * LANES)
        u_g = u[:, ls]
        acc = u_g
        for dlt in range(1, w):
            acc = acc + ext_ref[hp - dlt:hp - dlt + tm, ls]
        cnt = jnp.minimum(pos + 1, w).astype(F32)
        o_ref[0, :, ls] = _pool_groups(u_g, acc, cnt, w_ref, sc_ref, g).astype(o_ref.dtype)
    tail = ext_ref[tm:tm + hp, :]
    hist_ref[0] = tail
    ext_ref[0:hp, :] = tail


def _pool_prompt(u3, w_mix, scale, tm):
    b, s, d = u3.shape
    hp = POOL_HIST + 1
    return pl.pallas_call(
        functools.partial(_pool_prompt_kernel, tm=tm),
        grid=(b, s // tm),
        in_specs=[pl.BlockSpec((1, tm, d), lambda bi, si: (bi, si, 0)),
                  _full(w_mix.shape), _full(scale.shape)],
        out_specs=(pl.BlockSpec((1, tm, d), lambda bi, si: (bi, si, 0)),
                   pl.BlockSpec((1, hp, d), lambda bi, si: (bi, 0, 0))),
        out_shape=(jax.ShapeDtypeStruct((b, s, d), BF16),
                   jax.ShapeDtypeStruct((b, hp, d), F32)),
        scratch_shapes=[pltpu.VMEM((hp + tm, d), F32)],
        compiler_params=_params(("arbitrary", "arbitrary")),
        name="pool_prompt",
    )(u3, w_mix, scale)


def _pool_sample_kernel(u_ref, st_ref, w_ref, sc_ref, o_ref, ext_ref, *, pos0):
    nb, t, d = u_ref.shape
    hp = POOL_HIST + 1
    ext_ref[:, 0:hp, :] = st_ref[...]
    ext_ref[:, hp:hp + t, :] = u_ref[...]
    pos = pos0 + lax.broadcasted_iota(jnp.int32, (1, t, 1), 1)
    for g, w in enumerate(POOL_WINDOWS):
        ls = slice(g * LANES, (g + 1) * LANES)
        u_g = u_ref[:, :, ls]
        acc = u_g
        for dlt in range(1, w):
            acc = acc + ext_ref[:, hp - dlt:hp - dlt + t, ls]
        cnt = jnp.minimum(pos + 1, w).astype(F32)
        m = (acc / cnt - u_g).reshape(nb * t, LANES)
        o = jnp.dot(m.astype(BF16), w_ref[g], preferred_element_type=F32) * sc_ref[:, ls]
        o_ref[:, ls] = o.astype(o_ref.dtype)


def _pool_sample(u3, state16, w_mix, scale, pos0):
    nb, t, d = u3.shape
    hp = POOL_HIST + 1
    return pl.pallas_call(
        functools.partial(_pool_sample_kernel, pos0=pos0),
        grid=(1,),
        in_specs=[_full(u3.shape), _full(state16.shape), _full(w_mix.shape), _full(scale.shape)],
        out_specs=_full((nb * t, d)),
        out_shape=jax.ShapeDtypeStruct((nb * t, d), BF16),
        scratch_shapes=[pltpu.VMEM((nb, hp + t, d), F32)],
        compiler_params=_params(("arbitrary",)),
        name="pool_sample",
    )(u3, state16, w_mix, scale)


def _upper_tri(n):
    r = lax.broadcasted_iota(jnp.int32, (n, n), 0)
    c = lax.broadcasted_iota(jnp.int32, (n, n), 1)
    return (r <= c).astype(F32)


def _cumsum_prompt_kernel(lf_ref, c_ref, carry_ref):
    @pl.when(pl.program_id(1) == 0)
    def _():
        carry_ref[...] = jnp.zeros_like(carry_ref)

    tc = lf_ref.shape[-1]
    c = carry_ref[...] + jnp.dot(lf_ref[0], _upper_tri(tc), precision=HIGHEST, preferred_element_type=F32)
    c_ref[0] = c
    carry_ref[...] = jnp.broadcast_to(c[:, tc - 1:tc], c.shape)


def _cumsum_prompt(lf_t, tc):
    b, h, s = lf_t.shape
    return pl.pallas_call(
        _cumsum_prompt_kernel,
        grid=(b, s // tc),
        in_specs=[pl.BlockSpec((1, h, tc), lambda bi, si: (bi, 0, si))],
        out_specs=pl.BlockSpec((1, h, tc), lambda bi, si: (bi, 0, si)),
        out_shape=jax.ShapeDtypeStruct((b, h, s), F32),
        scratch_shapes=[pltpu.VMEM((h, tc), F32)],
        compiler_params=_params(("arbitrary", "arbitrary")),
        name="cumsum_prompt",
    )(lf_t)


def _split3(x):
    hi = x.astype(BF16)
    r1 = x - hi.astype(F32)
    mid = r1.astype(BF16)
    lo = (r1 - mid.astype(F32)).astype(BF16)
    return hi, mid, lo


def _dot3(x, w_b):
    hi, mid, lo = _split3(x)
    d = lambda a: jnp.dot(a, w_b, preferred_element_type=F32)
    return d(hi) + d(mid) + d(lo)


def _augment_kernel(q_ref, k_ref, c_ref, qa_ref, ka_ref, *, n_heads):
    tm = q_ref.shape[1]
    lane = lax.broadcasted_iota(jnp.int32, (tm, LANES), 1)
    one = jnp.ones((tm, LANES), F32)
    zero = jnp.zeros((tm, LANES), F32)
    for h in range(n_heads):
        slab = slice((h // 2) * LANES, (h // 2 + 1) * LANES)
        lo_half = (h % 2) == 0
        spare0 = HEAD_DIM if lo_half else 0
        in_head = (lane < HEAD_DIM) if lo_half else (lane >= HEAD_DIM)
        c_col = jnp.broadcast_to(c_ref[0, :, h:h + 1], (tm, LANES))
        c_terms = [t.astype(F32) for t in _split3(c_col)]
        n_terms = [t.astype(F32) for t in _split3(-c_col)]

        def spare(vals):
            out = zero
            for i, v in enumerate(vals):
                out = jnp.where(lane == spare0 + i, v, out)
            return out

        qa_ref[0, h] = jnp.where(in_head, q_ref[0, :, slab].astype(F32), spare(c_terms + [one] * 3)).astype(BF16)
        ka_ref[0, h] = jnp.where(in_head, k_ref[0, :, slab].astype(F32), spare([one] * 3 + n_terms)).astype(BF16)


def _augment(qb, kb, c_cols, tm):
    b, s, d = qb.shape
    n_heads = d // HEAD_DIM
    row = lambda w: pl.BlockSpec((1, tm, w), lambda bi, si: (bi, si, 0))
    per_head = pl.BlockSpec((1, n_heads, tm, LANES), lambda bi, si: (bi, 0, si, 0))
    return pl.pallas_call(
        functools.partial(_augment_kernel, n_heads=n_heads),
        grid=(b, s // tm),
        in_specs=[row(d), row(d), row(n_heads)],
        out_specs=(per_head, per_head),
        out_shape=(jax.ShapeDtypeStruct((b, n_heads, s, LANES), BF16),
                   jax.ShapeDtypeStruct((b, n_heads, s, LANES), BF16)),
        compiler_params=_params(("arbitrary", "arbitrary")),
        name="attn_augment",
    )(qb, kb, c_cols)


def _attn_prompt_kernel(qa0_ref, qa1_ref, ka0_ref, ka1_ref, vt_ref, o_ref, m_ref, l_ref, acc_ref, *, tq):
    qi = pl.program_id(2)
    contract_last = (((1,), (1,)), ((), ()))
    qs = (qa0_ref[0, 0], qa1_ref[0, 0])
    ks = (ka0_ref, ka1_ref)
    m_ref[...] = jnp.full_like(m_ref, -jnp.inf)
    l_ref[...] = jnp.zeros_like(l_ref)
    acc_ref[...] = jnp.zeros_like(acc_ref)

    def block(j, masked):
        k0 = pl.multiple_of(j * tq, tq)
        vt = vt_ref[0, 0, :, pl.ds(k0, tq)]
        scores = [lax.dot_general(ks[hh][0, 0, pl.ds(k0, tq), :], qs[hh], contract_last,
                                  preferred_element_type=F32) for hh in range(2)]
        for hh in range(2):
            s = scores[hh]
            if masked:
                key = lax.broadcasted_iota(jnp.int32, (tq, tq), 0)
                qry = lax.broadcasted_iota(jnp.int32, (tq, tq), 1)
                s = jnp.where(key <= qry, s, -jnp.inf)
            m_prev = m_ref[hh]
            m_new = jnp.maximum(m_prev, jnp.max(s, axis=0, keepdims=True))
            alpha = jnp.exp(m_prev - m_new)
            p = jnp.exp(s - m_new)
            l_ref[hh] = alpha * l_ref[hh] + jnp.sum(p, axis=0, keepdims=True)
            acc_ref[hh] = alpha * acc_ref[hh] + jnp.dot(vt, p.astype(BF16), preferred_element_type=F32)
            m_ref[hh] = m_new

    def body(j, carry):
        block(j, False)
        return carry

    lax.fori_loop(0, qi, body, 0)
    block(qi, True)
    o_t = jnp.concatenate([acc_ref[0, 0:HEAD_DIM, :] / l_ref[0], acc_ref[1, HEAD_DIM:LANES, :] / l_ref[1]], axis=0)
    o_ref[0] = o_t.T.astype(o_ref.dtype)


def _attn_prompt(qa, ka, vt, tq):
    b, n_heads, s, _ = qa.shape
    q_spec = lambda hh: pl.BlockSpec((1, 1, tq, LANES), lambda bi, p, qi: (bi, 2 * p + hh, qi, 0))
    k_spec = lambda hh: pl.BlockSpec((1, 1, s, LANES), lambda bi, p, qi: (bi, 2 * p + hh, 0, 0))
    return pl.pallas_call(
        functools.partial(_attn_prompt_kernel, tq=tq),
        grid=(b, n_heads // 2, s // tq),
        in_specs=[q_spec(0), q_spec(1), k_spec(0), k_spec(1),
                  pl.BlockSpec((1, 1, LANES, s), lambda bi, p, qi: (bi, p, 0, 0))],
        out_specs=pl.BlockSpec((1, tq, LANES), lambda bi, p, qi: (bi, qi, p)),
        out_shape=jax.ShapeDtypeStruct((b, s, n_heads * HEAD_DIM), BF16),
        scratch_shapes=[pltpu.VMEM((2, 1, tq), F32), pltpu.VMEM((2, 1, tq), F32),
                        pltpu.VMEM((2, LANES, tq), F32)],
        compiler_params=_params(("arbitrary",) * 3),
        name="attn_prompt",
    )(qa, qa, ka, ka, vt)


def _page_sums_kernel(lf_ref, w_ref, sfx_ref, tot_ref):
    r = _dot3(lf_ref[...], w_ref[...])
    sfx_ref[...] = r[:, 0:PAGE]
    tot_ref[...] = r[:, PAGE:2 * PAGE]


def _page_sums(lf_rows, rows):
    n = lf_rows.shape[0]
    pos = np.arange(PAGE)
    w_b = jnp.asarray(np.concatenate([pos[:, None] > pos[None, :], np.ones((PAGE, PAGE), bool)], axis=1), BF16)
    blk = pl.BlockSpec((rows, PAGE), lambda i: (i, 0))
    return pl.pallas_call(
        _page_sums_kernel,
        grid=(n // rows,),
        in_specs=[blk, _full(w_b.shape)],
        out_specs=(blk, blk),
        out_shape=(jax.ShapeDtypeStruct((n, PAGE), F32), jax.ShapeDtypeStruct((n, PAGE), F32)),
        compiler_params=_params(("arbitrary",)),
        name="page_sums",
    )(lf_rows, w_b)


def _attn_sample_kernel(pt_ref, q_ref, kn_ref, vn_ref, lfn_ref, *rest, pps, n_heads, t_new):
    k_refs = rest[0:pps]
    v_refs = rest[pps:2 * pps]
    sfx_refs = rest[2 * pps:3 * pps]
    tot_refs = rest[3 * pps:4 * pps]
    o_ref = rest[4 * pps]
    qbd_ref, cq_ref, m_ref, l_ref, acc_ref, carry_ref = rest[4 * pps + 1:]
    j = pl.program_id(1)
    nj = pl.num_programs(1)
    d = n_heads * HEAD_DIM
    rows = t_new * n_heads
    contract_last = (((1,), (1,)), ((), ()))
    head_of_lane = lax.broadcasted_iota(jnp.int32, (n_heads, d), 1) // HEAD_DIM
    head_mask = head_of_lane == lax.broadcasted_iota(jnp.int32, (n_heads, d), 0)
    per_token = lambda tile: jnp.concatenate([tile] * t_new, axis=0)

    @pl.when(j == 0)
    def _():
        q = q_ref[0].astype(F32)
        cn = jnp.dot(lfn_ref[0], _upper_tri(PAGE), precision=HIGHEST, preferred_element_type=F32)
        for t in range(t_new):
            qt = jnp.broadcast_to(q[t:t + 1, :], (n_heads, d))
            qbd_ref[t * n_heads:(t + 1) * n_heads, :] = jnp.where(head_mask, qt, jnp.zeros_like(qt))
            cq_ref[t * n_heads:(t + 1) * n_heads, :] = jnp.broadcast_to(cn[:, t:t + 1], (n_heads, PAGE))
        pad = jnp.zeros((PAGE - t_new, d), F32)
        kb = jnp.concatenate([kn_ref[0], pad], axis=0).astype(BF16)
        vb = jnp.concatenate([vn_ref[0], pad], axis=0).astype(BF16)
        s = lax.dot_general(qbd_ref[...].astype(BF16), kb, contract_last, preferred_element_type=F32)
        s = s + (cq_ref[...] - per_token(cn))
        key = lax.broadcasted_iota(jnp.int32, (rows, PAGE), 1)
        tok = lax.broadcasted_iota(jnp.int32, (rows, PAGE), 0) // n_heads
        s = jnp.where(key <= tok, s, -jnp.inf)
        m = jnp.max(s, axis=1, keepdims=True)
        p = jnp.exp(s - m)
        m_ref[...] = m
        l_ref[...] = jnp.sum(p, axis=1, keepdims=True)
        acc_ref[...] = jnp.dot(p.astype(BF16), vb, preferred_element_type=F32)
        carry_ref[...] = jnp.zeros_like(carry_ref)

    qbd = qbd_ref[...].astype(BF16)
    cq = cq_ref[...]
    carry = carry_ref[...]
    m_prev = m_ref[...]
    m_new = m_prev
    scores = []
    for i in range(pps):
        kt = k_refs[i][0].reshape(d, PAGE).astype(BF16)
        s = jnp.dot(qbd, kt, preferred_element_type=F32)
        s = s + (cq + per_token(carry + sfx_refs[i][0]))
        scores.append(s)
        m_new = jnp.maximum(m_new, jnp.max(s, axis=1, keepdims=True))
        carry = carry + tot_refs[i][0]
    carry_ref[...] = carry
    alpha = jnp.exp(m_prev - m_new)
    l = alpha * l_ref[...]
    acc = alpha * acc_ref[...]
    for i in range(pps):
        p = jnp.exp(scores[i] - m_new)
        l = l + jnp.sum(p, axis=1, keepdims=True)
        vt = v_refs[i][0].reshape(d, PAGE).astype(BF16)
        acc = acc + lax.dot_general(p.astype(BF16), vt, contract_last, preferred_element_type=F32)
    m_ref[...] = m_new
    l_ref[...] = l
    acc_ref[...] = acc

    @pl.when(j == nj - 1)
    def _():
        o = acc / l
        out_rows = []
        for t in range(t_new):
            ot = jnp.where(head_mask, o[t * n_heads:(t + 1) * n_heads, :], 0.0)
            out_rows.append(jnp.sum(ot, axis=0, keepdims=True))
        o_ref[0] = jnp.concatenate(out_rows, axis=0)


def _attn_sample(page_table, qb, kt_cache, vt_cache, k_new, v_new, lf_new_t, sfx, tot, pps):
    db, t_new, d = qb.shape
    n_heads = kt_cache.shape[1]
    n_pages = page_table.shape[1]
    assert n_pages % pps == 0
    rows = t_new * n_heads
    per_seq = lambda b, j, pt: (b, 0, 0)

    def page4(i):
        return lambda b, j, pt: (pt[b, n_pages - 1 - (j * pps + i)], 0, 0, 0)

    def page3(i):
        return lambda b, j, pt: (pt[b, n_pages - 1 - (j * pps + i)], 0, 0)

    in_specs = [pl.BlockSpec((1, t_new, d), per_seq), pl.BlockSpec((1, t_new, d), per_seq),
                pl.BlockSpec((1, t_new, d), per_seq), pl.BlockSpec((1, n_heads, PAGE), per_seq)]
    in_specs += [pl.BlockSpec((1, n_heads, HEAD_DIM, PAGE), page4(i)) for i in range(pps)]
    in_specs += [pl.BlockSpec((1, n_heads, HEAD_DIM, PAGE), page4(i)) for i in range(pps)]
    in_specs += [pl.BlockSpec((1, n_heads, PAGE), page3(i)) for i in range(pps)]
    in_specs += [pl.BlockSpec((1, n_heads, PAGE), page3(i)) for i in range(pps)]
    grid_spec = pltpu.PrefetchScalarGridSpec(
        num_scalar_prefetch=1,
        grid=(db, n_pages // pps),
        in_specs=in_specs,
        out_specs=pl.BlockSpec((1, t_new, d), per_seq),
        scratch_shapes=[pltpu.VMEM((rows, d), F32), pltpu.VMEM((rows, PAGE), F32),
                        pltpu.VMEM((rows, 1), F32), pltpu.VMEM((rows, 1), F32),
                        pltpu.VMEM((rows, d), F32), pltpu.VMEM((n_heads, PAGE), F32)],
    )
    args = [page_table, qb, k_new, v_new, lf_new_t] + [kt_cache] * pps + [vt_cache] * pps + [sfx] * pps + [tot] * pps
    return pl.pallas_call(
        functools.partial(_attn_sample_kernel, pps=pps, n_heads=n_heads, t_new=t_new),
        grid_spec=grid_spec,
        out_shape=jax.ShapeDtypeStruct((db, t_new, d), F32),
        compiler_params=_params(("arbitrary", "arbitrary")),
        name="attn_sample",
    )(*args)


def _merge_kernel(x_ref, pool_ref, attn_ref, gmix_ref, wg_ref, wbp_ref, wba_ref, wo_ref, gffn_ref, wq_ref,
                  h_ref, hnt_ref, qp_ref):
    x = x_ref[...]
    dm = x.shape[1]
    ms = jnp.mean(x * x, axis=-1, keepdims=True)
    xn = (x * lax.rsqrt(ms + RMS_EPS) * gmix_ref[...]).astype(BF16)
    gates = jax.nn.sigmoid(jnp.dot(xn, wg_ref[...], preferred_element_type=F32))
    mixed = (gates[:, 0:dm] * jnp.dot(pool_ref[...], wbp_ref[...], preferred_element_type=F32)
             + gates[:, dm:2 * dm] * jnp.dot(attn_ref[...], wba_ref[...], preferred_element_type=F32))
    h = x + jnp.dot(mixed.astype(BF16), wo_ref[...], preferred_element_type=F32)
    h_ref[...] = h
    hms = jnp.mean(h * h, axis=-1, keepdims=True)
    hn = h * lax.rsqrt(hms + RMS_EPS) * gffn_ref[...]
    hnb = hn.astype(BF16)
    hnt_ref[...] = hn.T.astype(BF16)
    qp_ref[...] = jnp.dot(hnb, wq_ref[...], preferred_element_type=F32).astype(BF16)


def _merge(x, pool, attn, g_mix, w_gates, wbp, wba, wo, g_ffn, wq, tm):
    n, dm = x.shape
    dq = wq.shape[1]
    row = lambda w: pl.BlockSpec((tm, w), lambda i: (i, 0))
    return pl.pallas_call(
        _merge_kernel,
        grid=(n // tm,),
        in_specs=[row(dm), row(pool.shape[1]), row(attn.shape[1]), _full(g_mix.shape), _full(w_gates.shape),
                  _full(wbp.shape), _full(wba.shape), _full(wo.shape), _full(g_ffn.shape), _full(wq.shape)],
        out_specs=(row(dm), pl.BlockSpec((dm, tm), lambda i: (0, i)), row(dq)),
        out_shape=(jax.ShapeDtypeStruct((n, dm), F32),
                   jax.ShapeDtypeStruct((dm, n), BF16),
                   jax.ShapeDtypeStruct((n, dq), BF16)),
        compiler_params=_params(("arbitrary",)),
        name="merge",
    )(x, pool, attn, g_mix, w_gates, wbp, wba, wo, g_ffn, wq)


def _oddeven_merge_sort_pairs(n):
    pairs = []
    p = 1
    while p < n:
        k = p
        while k >= 1:
            for j in range(k % p, n - k, 2 * k):
                for i in range(min(k, n - j - k)):
                    if (i + j) // (2 * p) == (i + j + k) // (2 * p):
                        pairs.append((i + j, i + j + k))
            k //= 2
        p *= 2
    return pairs


def _bitonic_merge_pairs(n):
    pairs = []
    k = n // 2
    while k >= 1:
        for i in range(n):
            if (i // k) % 2 == 0 and i + k < n:
                pairs.append((i, i + k))
        k //= 2
    return pairs


_SORT16 = _oddeven_merge_sort_pairs(16)
_MERGE16 = _bitonic_merge_pairs(16)


def _apply_desc(vals, pairs):
    vals = list(vals)
    for a, b in pairs:
        hi = jnp.maximum(vals[a], vals[b])
        lo = jnp.minimum(vals[a], vals[b])
        vals[a], vals[b] = hi, lo
    return vals


def _top16_across_sublanes(vals):
    for shift in (4, 2, 1):
        other = [pltpu.roll(v, shift, axis=0) for v in vals]
        vals = [jnp.maximum(vals[i], other[PEER_TOPK - 1 - i]) for i in range(PEER_TOPK)]
        vals = _apply_desc(vals, _MERGE16)
    return vals


def _max_below(vals, limit):
    m = jnp.where(vals[0] < limit, vals[0], -jnp.inf)
    for v in vals[1:]:
        m = jnp.maximum(m, jnp.where(v < limit, v, -jnp.inf))
    for shift in (4, 2, 1):
        m = jnp.maximum(m, pltpu.roll(m, shift, axis=0))
    return m


def _topk_tile(s_tile):
    vals = [s_tile[SUBLANES * i:SUBLANES * (i + 1), :] for i in range(N_KEYS // SUBLANES)]
    top = _top16_across_sublanes(_apply_desc(vals, _SORT16))
    return top, _max_below(vals, top[PEER_TOPK - 1])


def _route_kernel(qp_ref, sk_ref, bound_ref, p1_ref, s2_ref, e2_ref, s1_ref, *, n_heads, lt):
    tn = qp_ref.shape[0]
    neg = -jnp.inf
    sub = lax.broadcasted_iota(jnp.int32, (SUBLANES, lt), 0)
    for h in range(n_heads):
        for half, dst in ((0, s1_ref), (1, s2_ref)):
            c0 = (2 * h + half) * LANES
            dst[h] = lax.dot_general(sk_ref[2 * h + half], qp_ref[:, c0:c0 + LANES],
                                     (((1,), (1,)), ((), ())), preferred_element_type=F32)

    def body(it, carry):
        h = it // (tn // lt)
        lanes = pl.ds(pl.multiple_of((it % (tn // lt)) * lt, lt), lt)
        s1_tile = s1_ref[h, :, lanes]
        s2_tile = s2_ref[h, :, lanes]
        a, a_next = _topk_tile(s1_tile)
        b, b_next = _topk_tile(s2_tile)
        def pack(vs):
            out = vs[0]
            for i in range(1, len(vs)):
                out = jnp.where(sub == i, vs[i], out)
            return out
        b_lo, b_hi, a_hi = pack(b[0:8]), pack(b[8:16]), pack(a[8:16])
        cands = [a[0] + b_lo, a[0] + b_hi, a[1] + b_lo]
        for k in range(2, 8):
            cands.append(jnp.where(sub < PEER_TOPK // (k + 1), a[k] + b_lo, neg))
        cands.append(a_hi + b[0])
        padded = cands + [jnp.full((SUBLANES, lt), neg, F32)] * (PEER_TOPK - len(cands))
        top = _top16_across_sublanes(_apply_desc(padded, _SORT16))
        z = jnp.zeros((SUBLANES, lt), F32)
        for i in range(PEER_TOPK):
            z = z + jnp.exp(top[i] - top[0])
        t17 = jnp.maximum(_max_below(cands, top[PEER_TOPK - 1]), jnp.maximum(a[0] + b_next, a_next + b[0]))
        tau = 0.5 * (top[PEER_TOPK - 1] + t17)
        half_inv_z = 0.5 / z
        tiles = lambda t: [t[SUBLANES * i:SUBLANES * (i + 1), :] for i in range(N_KEYS // SUBLANES)]
        stack = lambda vs: jnp.concatenate(vs, axis=0)
        bound_ref[h, :, lanes] = stack([tau - x for x in tiles(s1_tile)])
        p1_ref[h, :, lanes] = stack([jnp.exp(x - a[0]) * half_inv_z for x in tiles(s1_tile)])
        e2_ref[h, :, lanes] = stack([jnp.exp(x - b[0]) for x in tiles(s2_tile)])
        return carry

    lax.fori_loop(0, n_heads * (tn // lt), body, 0)


def _route(qp, sub_keys_b, n_heads, tn, lt=LANES):
    n, dq = qp.shape
    blk = pl.BlockSpec((n_heads, N_KEYS, tn), lambda i: (0, 0, i))
    arr = jax.ShapeDtypeStruct((n_heads, N_KEYS, n), F32)
    return pl.pallas_call(
        functools.partial(_route_kernel, n_heads=n_heads, lt=lt),
        grid=(n // tn,),
        in_specs=[pl.BlockSpec((tn, dq), lambda i: (i, 0)), _full(sub_keys_b.shape)],
        out_specs=(blk, blk, blk, blk),
        out_shape=(arr, arr, arr, arr),
        scratch_shapes=[pltpu.VMEM((n_heads, N_KEYS, tn), F32)],
        compiler_params=_params(("arbitrary",)),
        name="peer_route",
    )(qp, sub_keys_b)


def _tree_sum(terms):
    while len(terms) > 1:
        terms = [terms[i] + terms[i + 1] for i in range(0, len(terms) - 1, 2)] + \
                ([terms[-1]] if len(terms) % 2 else [])
    return terms[0]


def _peer_kernel(hnt_ref, bound_ref, p1_ref, s2_ref, e2_ref, ua_ref, ub_ref, vta_ref, vtb_ref, h_ref, y_ref,
                 se_ref, a_ref, wg_ref, acc_ref, *, n_heads, rows_per_step):
    s = pl.program_id(1)
    n_blocks = pl.num_programs(1) - 1
    tt = hnt_ref.shape[1]
    n_lt = tt // LANES
    n_jv = N_KEYS // SUBLANES
    half = tt // 2
    first, second = slice(0, half), slice(half, tt)
    lane_tile = lambda tl: slice(tl * LANES, (tl + 1) * LANES)

    def activations(u_rows, cols):
        a_ref[:, cols] = jnp.dot(u_rows[...], hnt_ref[:, cols], preferred_element_type=F32)

    def accumulate(vt_rows, cols):
        acc_ref[:, cols] += jnp.dot(vt_rows[...], wg_ref[:, cols].astype(BF16), preferred_element_type=F32)

    @pl.when(s == 0)
    def _():
        acc_ref[...] = jnp.zeros_like(acc_ref)
        wg_ref[:, second] = jnp.zeros((wg_ref.shape[0], half), F32)
        for h in range(n_heads):
            for tl in range(n_lt):
                se_ref[tl, :, h, 0] = s2_ref[h, :, lane_tile(tl)].reshape(n_jv, SUBLANES, LANES)
                se_ref[tl, :, h, 1] = e2_ref[h, :, lane_tile(tl)].reshape(n_jv, SUBLANES, LANES)
        activations(ub_ref, first)

    def key_tile(tl, jv):
        s2 = [se_ref[tl, jv, h, 0] for h in range(n_heads)]
        e2 = [se_ref[tl, jv, h, 1] for h in range(n_heads)]
        row_tile = lambda ref, h, r: jnp.broadcast_to(ref[h, r:r + 1, lane_tile(tl)], (SUBLANES, LANES))
        for r in range(rows_per_step):
            terms = [jnp.where(s2[h] >= row_tile(bound_ref, h, r), e2[h] * row_tile(p1_ref, h, r), 0.0)
                     for h in range(n_heads)]
            rows = slice(r * N_KEYS + jv * SUBLANES, r * N_KEYS + (jv + 1) * SUBLANES)
            act = a_ref[rows, lane_tile(tl)]
            gelu = act * (1.0 + lax.erf(act * np.float32(1.0 / np.sqrt(2.0))))
            wg_ref[rows, lane_tile(tl)] = _tree_sum(terms) * gelu

    def routing_weights(lane_tiles):
        for tl in lane_tiles:
            for jv in range(n_jv):
                key_tile(tl, jv)

    @pl.when(s >= 1)
    def _():
        activations(ua_ref, second)
        accumulate(vta_ref, second)
        routing_weights(range(0, n_lt // 2))
        accumulate(vtb_ref, first)
        activations(ub_ref, first)
        routing_weights(range(n_lt // 2, n_lt))

    @pl.when(s == n_blocks)
    def _():
        accumulate(vtb_ref, second)
        y_ref[...] = h_ref[...] + acc_ref[...].T


def _peer(hnt, bound, p1, s2, e2, u_b, vt_b, h, tt, rows_per_step):
    dm, n = hnt.shape
    n_heads = s2.shape[0]
    n_exp = u_b.shape[0]
    eb = rows_per_step * N_KEYS
    nb = n_exp // eb
    blk = lambda off: (lambda t, s: jnp.clip(s + off, 0, nb - 1))
    u_spec = lambda off: pl.BlockSpec((eb, dm), lambda t, s: (blk(off)(t, s), 0))
    vt_spec = lambda off: pl.BlockSpec((dm, eb), lambda t, s: (0, blk(off)(t, s)))
    row_spec = pl.BlockSpec((n_heads, rows_per_step, tt), lambda t, s: (0, blk(-1)(t, s), t))
    all_keys = pl.BlockSpec((n_heads, N_KEYS, tt), lambda t, s: (0, 0, t))
    return pl.pallas_call(
        functools.partial(_peer_kernel, n_heads=n_heads, rows_per_step=rows_per_step),
        grid=(n // tt, nb + 1),
        in_specs=[
            pl.BlockSpec((dm, tt), lambda t, s: (0, t)),
            row_spec, row_spec, all_keys, all_keys,
            u_spec(-1), u_spec(0), vt_spec(-2), vt_spec(-1),
            pl.BlockSpec((tt, dm), lambda t, s: (t, 0)),
        ],
        out_specs=pl.BlockSpec((tt, dm), lambda t, s: (t, 0)),
        out_shape=jax.ShapeDtypeStruct((n, dm), F32),
        scratch_shapes=[pltpu.VMEM((tt // LANES, N_KEYS // SUBLANES, n_heads, 2, SUBLANES, LANES), F32),
                        pltpu.VMEM((eb, tt), F32), pltpu.VMEM((eb, tt), F32), pltpu.VMEM((dm, tt), F32)],
        compiler_params=_params(("arbitrary", "arbitrary")),
        name="peer_dense",
    )(hnt, bound, p1, s2, e2, u_b, u_b, vt_b, vt_b, h)


def _layer(x_p, x_s, cache_k, cache_v, cache_logf, state_pool, page_table, norm_mix_g, w_in, b_forget,
           q_norm_g, k_norm_g, pool_mix_w, pool_scale, w_branch_pool, w_branch_attn, w_out, norm_ffn_g,
           w_query, sub_keys, peer_u, peer_v):
    b, s, dm = x_p.shape
    db, t_new, _ = x_s.shape
    n_heads = b_forget.shape[0]
    d = n_heads * HEAD_DIM
    n_p, n_s = b * s, db * t_new
    past_len = page_table.shape[1] * PAGE
    tm = ROW_TILE

    w_a = jnp.concatenate([w_in[:, 0:4 * d + n_heads],
                           jnp.zeros((dm, LANES - n_heads), w_in.dtype)], axis=1).astype(BF16)
    w_gates = w_in[:, 4 * d + n_heads:].astype(BF16)
    b_f = jnp.concatenate([b_forget, jnp.zeros((LANES - n_heads,), F32)]).reshape(1, LANES)
    qg = jnp.tile(q_norm_g, n_heads).reshape(1, d)
    kg = jnp.tile(k_norm_g, n_heads).reshape(1, d)
    lane_head = np.arange(d) // HEAD_DIM
    gmat = jnp.asarray(lane_head[:, None] == lane_head[None, :], BF16)
    g_mix = norm_mix_g.reshape(1, dm)
    g_ffn = norm_ffn_g.reshape(1, dm)
    w_mix = pool_mix_w.astype(BF16)
    p_scale = pool_scale.reshape(1, d)

    wbp, wba, wo, wq = (w.astype(BF16) for w in (w_branch_pool, w_branch_attn, w_out, w_query))
    n_ph = sub_keys.shape[0]
    sk_b = sub_keys.reshape(n_ph * 2, N_KEYS, sub_keys.shape[-1]).astype(BF16)
    u_b, vt_b = peer_u.astype(BF16), jnp.transpose(peer_v).astype(BF16)

    def merge_and_ffn(x2, pool, attn):
        h, hnt, qp = _merge(x2, pool, attn, g_mix, w_gates, wbp, wba, wo, g_ffn, wq, tm)
        bound, p1, s2, e2 = _route(qp, sk_b, n_ph, ROUTE_TILE)
        return _peer(hnt, bound, p1, s2, e2, u_b, vt_b, h, PEER_TOKENS, PEER_ROWS)

    x2_p = x_p.reshape(n_p, dm)
    u, qb, kt, kb, vt, vtb, lf = _project(x2_p, g_mix, w_a, b_f, qg, kg, gmat, n_heads, tm, seq_len=s)
    pool_p, hist16 = _pool_prompt(u.reshape(b, s, d), w_mix, p_scale, tm)
    hist_p = hist16[:, 1:, :]
    lf_p = lf.reshape(b, s, n_heads)
    c_t = _cumsum_prompt(jnp.transpose(lf_p, (0, 2, 1)), CUMSUM_TILE)
    qa, ka = _augment(qb.reshape(b, s, d), kb.reshape(b, s, d), jnp.transpose(c_t, (0, 2, 1)), tm)
    attn_p = _attn_prompt(qa, ka, vtb.reshape(b, n_heads // 2, LANES, s), ATTN_TILE)
    y_p = merge_and_ffn(x2_p, pool_p.reshape(n_p, d), attn_p.reshape(n_p, d)).reshape(b, s, dm)
    k_p = jnp.transpose(kt.reshape(b, n_heads, HEAD_DIM, s), (0, 3, 1, 2))
    v_p = jnp.transpose(vt.reshape(b, n_heads, HEAD_DIM, s), (0, 3, 1, 2))

    x2_s = x_s.reshape(n_s, dm)
    u, qb, k, _, v, _, lf = _project(x2_s, g_mix, w_a, b_f, qg, kg, gmat, n_heads, tm)
    u_s = u.reshape(db, t_new, d)
    state16 = jnp.concatenate([jnp.zeros((db, 1, d), F32), state_pool], axis=1)
    pool_s = _pool_sample(u_s, state16, w_mix, p_scale, past_len)
    hist_s = jnp.concatenate([state_pool, u_s], axis=1)[:, -POOL_HIST:, :]
    lf_s = lf.reshape(db, t_new, n_heads)
    n_phys = cache_k.shape[0]
    lf_rows = jnp.transpose(cache_logf, (0, 2, 1)).reshape(n_phys * n_heads, PAGE)
    sfx, tot = _page_sums(lf_rows, PAGE_SUM_ROWS)
    lf_new_t = jnp.concatenate([jnp.transpose(lf_s, (0, 2, 1)),
                                jnp.zeros((db, n_heads, PAGE - t_new), F32)], axis=2)
    attn_s = _attn_sample(page_table, qb.reshape(db, t_new, d),
                          jnp.transpose(cache_k, (0, 2, 3, 1)), jnp.transpose(cache_v, (0, 2, 3, 1)),
                          k.reshape(db, t_new, d), v.reshape(db, t_new, d), lf_new_t,
                          sfx.reshape(n_phys, n_heads, PAGE), tot.reshape(n_phys, n_heads, PAGE),
                          PAGES_PER_STEP)
    y_s = merge_and_ffn(x2_s, pool_s, attn_s.reshape(n_s, d).astype(BF16)).reshape(db, t_new, dm)
    k_s = k.reshape(db, t_new, n_heads, HEAD_DIM)
    v_s = v.reshape(db, t_new, n_heads, HEAD_DIM)
    return y_p, y_s, k_p, v_p, lf_p, hist_p, k_s, v_s, lf_s, hist_s


def kernel(x_prompt, x_sample, cache_k, cache_v, cache_logf, state_pool, page_table, norm_mix_g, w_in,
           b_forget, q_norm_g, k_norm_g, pool_mix_w, pool_scale, w_branch_pool, w_branch_attn, w_out,
           norm_ffn_g, w_query, sub_keys, peer_u, peer_v):
    depth = w_in.shape[0]
    x_p, x_s = x_prompt, x_sample
    outs = [[] for _ in range(8)]
    for l in range(depth):
        res = _layer(x_p, x_s, cache_k[l], cache_v[l], cache_logf[l], state_pool[l], page_table,
                     norm_mix_g[l], w_in[l], b_forget[l], q_norm_g[l], k_norm_g[l], pool_mix_w[l],
                     pool_scale[l], w_branch_pool[l], w_branch_attn[l], w_out[l], norm_ffn_g[l],
                     w_query[l], sub_keys[l], peer_u[l], peer_v[l])
        x_p, x_s = res[0], res[1]
        for acc, r in zip(outs, res[2:]):
            acc.append(r)
    return (x_p, x_s) + tuple(jnp.stack(o) for o in outs)
```

```python
import functools

import numpy as np
import jax
import jax.numpy as jnp
from jax import lax
from jax.experimental import pallas as pl
from jax.experimental.pallas import tpu as pltpu

F32 = jnp.float32
BF16 = jnp.bfloat16

LANES = 128
SUBLANES = 8
VMEM_LIMIT = 56 * 1024 * 1024

HEAD_DIM = 64
POOL_WINDOWS = (2, 4, 8, 16)
POOL_HIST = 15
PAGE = 128
N_KEYS = 128
PEER_TOPK = 16
RMS_EPS = 1e-6
ATTN_SCALE = HEAD_DIM ** -0.5
HIGHEST = lax.Precision.HIGHEST

ROW_TILE = 512
CUMSUM_TILE = 512
ATTN_TILE = 1024
PAGES_PER_STEP = 16
PAGE_SUM_ROWS = 512
ROUTE_TILE = 512
PEER_TOKENS = 512
PEER_ROWS = 8


def _params(sem, vmem=VMEM_LIMIT):
    return pltpu.CompilerParams(dimension_semantics=sem, vmem_limit_bytes=vmem)


def _full(shape):
    n = len(shape)
    return pl.BlockSpec(shape, lambda *_: (0,) * n)


def _log_sigmoid(z):
    return jnp.minimum(z, 0.0) - jnp.log1p(jnp.exp(-jnp.abs(z)))


def _group_mean_sq(x, gmat):
    xx = x * x
    hi = xx.astype(BF16)
    lo = (xx - hi.astype(F32)).astype(BF16)
    s = jnp.dot(hi, gmat, preferred_element_type=F32) + jnp.dot(lo, gmat, preferred_element_type=F32)
    return s * (1.0 / HEAD_DIM)


def _proj_kernel(x_ref, g_ref, w_ref, bf_ref, qg_ref, kg_ref, gm_ref,
                 u_ref, qb_ref, k_ref, kb_ref, v_ref, vb_ref, lf_ref, *, kv_transposed):
    x = x_ref[...]
    ms = jnp.mean(x * x, axis=-1, keepdims=True)
    xn = (x * lax.rsqrt(ms + RMS_EPS) * g_ref[...]).astype(BF16)
    p = jnp.dot(xn, w_ref[...], preferred_element_type=F32)
    d = u_ref.shape[-1]
    u = p[:, 0:d]
    q = p[:, d:2 * d]
    k = p[:, 2 * d:3 * d]
    v = p[:, 3 * d:4 * d]
    f = p[:, 4 * d:4 * d + LANES]
    gm = gm_ref[...]
    qn = q * lax.rsqrt(_group_mean_sq(q, gm) + RMS_EPS) * qg_ref[...]
    kn = k * lax.rsqrt(_group_mean_sq(k, gm) + RMS_EPS) * kg_ref[...]
    u_ref[...] = u
    qb_ref[...] = (qn * ATTN_SCALE).astype(BF16)
    kb_ref[...] = kn.astype(BF16)
    if kv_transposed:
        k_ref[0] = kn.T
        v_t = v.T
        v_ref[0] = v_t
        vb_ref[0] = v_t.astype(BF16)
    else:
        k_ref[...] = kn
        v_ref[...] = v
        vb_ref[...] = v.astype(BF16)
    lf = _log_sigmoid(f + bf_ref[...])
    lf_ref[...] = lf[:, 0:lf_ref.shape[-1]]


def _project(x, g_mix, w_a, b_f, qg, kg, gmat, n_heads, tm, seq_len=None):
    n, dm = x.shape
    d = n_heads * HEAD_DIM
    row = lambda w: pl.BlockSpec((tm, w), lambda i: (i, 0))
    rows = lambda dt: jax.ShapeDtypeStruct((n, d), dt)
    kv_spec, kv = row(d), rows
    if seq_len is not None:
        tiles = seq_len // tm
        kv_spec = pl.BlockSpec((1, d, tm), lambda i: (i // tiles, 0, i % tiles))
        kv = lambda dt: jax.ShapeDtypeStruct((n // seq_len, d, seq_len), dt)
    return pl.pallas_call(
        functools.partial(_proj_kernel, kv_transposed=seq_len is not None),
        grid=(n // tm,),
        in_specs=[row(dm), _full(g_mix.shape), _full(w_a.shape), _full(b_f.shape),
                  _full(qg.shape), _full(kg.shape), _full(gmat.shape)],
        out_specs=(row(d), row(d), kv_spec, row(d), kv_spec, kv_spec, row(n_heads)),
        out_shape=(rows(F32), rows(BF16), kv(F32), rows(BF16), kv(F32), kv(BF16),
                   jax.ShapeDtypeStruct((n, n_heads), F32)),
        compiler_params=_params(("arbitrary",)),
        name="proj_in",
    )(x, g_mix, w_a, b_f, qg, kg, gmat)


def _pool_groups(u_g, win_sum, cnt, w_ref, sc_ref, g):
    m = win_sum / cnt - u_g
    o = jnp.dot(m.astype(BF16), w_ref[g], preferred_element_type=F32)
    return o * sc_ref[:, g * LANES:(g + 1) * LANES]


def _pool_prompt_kernel(u_ref, w_ref, sc_ref, o_ref, hist_ref, ext_ref, *, tm):
    si = pl.program_id(1)
    hp = POOL_HIST + 1

    @pl.when(si == 0)
    def _():
        ext_ref[0:hp, :] = jnp.zeros((hp, ext_ref.shape[1]), F32)

    u = u_ref[0]
    ext_ref[hp:hp + tm, :] = u
    pos = si * tm + lax.broadcasted_iota(jnp.int32, (tm, 1), 0)
    for g, w in enumerate(POOL_WINDOWS):
        ls = slice(g * LANES, (g + 1) * LANES)
        u_g = u[:, ls]
        acc = u_g
        for dlt in range(1, w):
            acc = acc + ext_ref[hp - dlt:hp - dlt + tm, ls]
        cnt = jnp.minimum(pos + 1, w).astype(F32)
        o_ref[0, :, ls] = _pool_groups(u_g, acc, cnt, w_ref, sc_ref, g).astype(o_ref.dtype)
    tail = ext_ref[tm:tm + hp, :]
    hist_ref[0] = tail
    ext_ref[0:hp, :] = tail


def _pool_prompt(u3, w_mix, scale, tm):
    b, s, d = u3.shape
    hp = POOL_HIST + 1
    return pl.pallas_call(
        functools.partial(_pool_prompt_kernel, tm=tm),
        grid=(b, s // tm),
        in_specs=[pl.BlockSpec((1, tm, d), lambda bi, si: (bi, si, 0)),
                  _full(w_mix.shape), _full(scale.shape)],
        out_specs=(pl.BlockSpec((1, tm, d), lambda bi, si: (bi, si, 0)),
                   pl.BlockSpec((1, hp, d), lambda bi, si: (bi, 0, 0))),
        out_shape=(jax.ShapeDtypeStruct((b, s, d), BF16),
                   jax.ShapeDtypeStruct((b, hp, d), F32)),
        scratch_shapes=[pltpu.VMEM((hp + tm, d), F32)],
        compiler_params=_params(("arbitrary", "arbitrary")),
        name="pool_prompt",
    )(u3, w_mix, scale)


def _pool_sample_kernel(u_ref, st_ref, w_ref, sc_ref, o_ref, ext_ref, *, pos0):
    nb, t, d = u_ref.shape
    hp = POOL_HIST + 1
    ext_ref[:, 0:hp, :] = st_ref[...]
    ext_ref[:, hp:hp + t, :] = u_ref[...]
    pos = pos0 + lax.broadcasted_iota(jnp.int32, (1, t, 1), 1)
    for g, w in enumerate(POOL_WINDOWS):
        ls = slice(g * LANES, (g + 1) * LANES)
        u_g = u_ref[:, :, ls]
        acc = u_g
        for dlt in range(1, w):
            acc = acc + ext_ref[:, hp - dlt:hp - dlt + t, ls]
        cnt = jnp.minimum(pos + 1, w).astype(F32)
        m = (acc / cnt - u_g).reshape(nb * t, LANES)
        o = jnp.dot(m.astype(BF16), w_ref[g], preferred_element_type=F32) * sc_ref[:, ls]
        o_ref[:, ls] = o.astype(o_ref.dtype)


def _pool_sample(u3, state16, w_mix, scale, pos0):
    nb, t, d = u3.shape
    hp = POOL_HIST + 1
    return pl.pallas_call(
        functools.partial(_pool_sample_kernel, pos0=pos0),
        grid=(1,),
        in_specs=[_full(u3.shape), _full(state16.shape), _full(w_mix.shape), _full(scale.shape)],
        out_specs=_full((nb * t, d)),
        out_shape=jax.ShapeDtypeStruct((nb * t, d), BF16),
        scratch_shapes=[pltpu.VMEM((nb, hp + t, d), F32)],
        compiler_params=_params(("arbitrary",)),
        name="pool_sample",
    )(u3, state16, w_mix, scale)


def _upper_tri(n):
    r = lax.broadcasted_iota(jnp.int32, (n, n), 0)
    c = lax.broadcasted_iota(jnp.int32, (n, n), 1)
    return (r <= c).astype(F32)


def _cumsum_prompt_kernel(lf_ref, c_ref, carry_ref):
    @pl.when(pl.program_id(1) == 0)
    def _():
        carry_ref[...] = jnp.zeros_like(carry_ref)

    tc = lf_ref.shape[-1]
    c = carry_ref[...] + jnp.dot(lf_ref[0], _upper_tri(tc), precision=HIGHEST, preferred_element_type=F32)
    c_ref[0] = c
    carry_ref[...] = jnp.broadcast_to(c[:, tc - 1:tc], c.shape)


def _cumsum_prompt(lf_t, tc):
    b, h, s = lf_t.shape
    return pl.pallas_call(
        _cumsum_prompt_kernel,
        grid=(b, s // tc),
        in_specs=[pl.BlockSpec((1, h, tc), lambda bi, si: (bi, 0, si))],
        out_specs=pl.BlockSpec((1, h, tc), lambda bi, si: (bi, 0, si)),
        out_shape=jax.ShapeDtypeStruct((b, h, s), F32),
        scratch_shapes=[pltpu.VMEM((h, tc), F32)],
        compiler_params=_params(("arbitrary", "arbitrary")),
        name="cumsum_prompt",
    )(lf_t)


def _split3(x):
    hi = x.astype(BF16)
    r1 = x - hi.astype(F32)
    mid = r1.astype(BF16)
    lo = (r1 - mid.astype(F32)).astype(BF16)
    return hi, mid, lo


def _dot3(x, w_b):
    hi, mid, lo = _split3(x)
    d = lambda a: jnp.dot(a, w_b, preferred_element_type=F32)
    return d(hi) + d(mid) + d(lo)


def _augment_kernel(q_ref, k_ref, c_ref, qa_ref, ka_ref, *, n_heads):
    tm = q_ref.shape[1]
    lane = lax.broadcasted_iota(jnp.int32, (tm, LANES), 1)
    one = jnp.ones((tm, LANES), F32)
    zero = jnp.zeros((tm, LANES), F32)
    for h in range(n_heads):
        slab = slice((h // 2) * LANES, (h // 2 + 1) * LANES)
        lo_half = (h % 2) == 0
        spare0 = HEAD_DIM if lo_half else 0
        in_head = (lane < HEAD_DIM) if lo_half else (lane >= HEAD_DIM)
        c_col = jnp.broadcast_to(c_ref[0, :, h:h + 1], (tm, LANES))
        c_terms = [t.astype(F32) for t in _split3(c_col)]
        n_terms = [t.astype(F32) for t in _split3(-c_col)]

        def spare(vals):
            out = zero
            for i, v in enumerate(vals):
                out = jnp.where(lane == spare0 + i, v, out)
            return out

        qa_ref[0, h] = jnp.where(in_head, q_ref[0, :, slab].astype(F32), spare(c_terms + [one] * 3)).astype(BF16)
        ka_ref[0, h] = jnp.where(in_head, k_ref[0, :, slab].astype(F32), spare([one] * 3 + n_terms)).astype(BF16)


def _augment(qb, kb, c_cols, tm):
    b, s, d = qb.shape
    n_heads = d // HEAD_DIM
    row = lambda w: pl.BlockSpec((1, tm, w), lambda bi, si: (bi, si, 0))
    per_head = pl.BlockSpec((1, n_heads, tm, LANES), lambda bi, si: (bi, 0, si, 0))
    return pl.pallas_call(
        functools.partial(_augment_kernel, n_heads=n_heads),
        grid=(b, s // tm),
        in_specs=[row(d), row(d), row(n_heads)],
        out_specs=(per_head, per_head),
        out_shape=(jax.ShapeDtypeStruct((b, n_heads, s, LANES), BF16),
                   jax.ShapeDtypeStruct((b, n_heads, s, LANES), BF16)),
        compiler_params=_params(("arbitrary", "arbitrary")),
        name="attn_augment",
    )(qb, kb, c_cols)


def _attn_prompt_kernel(qa0_ref, qa1_ref, ka0_ref, ka1_ref, vt_ref, o_ref, m_ref, l_ref, acc_ref, *, tq):
    qi = pl.program_id(2)
    contract_last = (((1,), (1,)), ((), ()))
    qs = (qa0_ref[0, 0], qa1_ref[0, 0])
    ks = (ka0_ref, ka1_ref)
    m_ref[...] = jnp.full_like(m_ref, -jnp.inf)
    l_ref[...] = jnp.zeros_like(l_ref)
    acc_ref[...] = jnp.zeros_like(acc_ref)

    def block(j, masked):
        k0 = pl.multiple_of(j * tq, tq)
        vt = vt_ref[0, 0, :, pl.ds(k0, tq)]
        scores = [lax.dot_general(ks[hh][0, 0, pl.ds(k0, tq), :], qs[hh], contract_last,
                                  preferred_element_type=F32) for hh in range(2)]
        for hh in range(2):
            s = scores[hh]
            if masked:
                key = lax.broadcasted_iota(jnp.int32, (tq, tq), 0)
                qry = lax.broadcasted_iota(jnp.int32, (tq, tq), 1)
                s = jnp.where(key <= qry, s, -jnp.inf)
            m_prev = m_ref[hh]
            m_new = jnp.maximum(m_prev, jnp.max(s, axis=0, keepdims=True))
            alpha = jnp.exp(m_prev - m_new)
            p = jnp.exp(s - m_new)
            l_ref[hh] = alpha * l_ref[hh] + jnp.sum(p, axis=0, keepdims=True)
            acc_ref[hh] = alpha * acc_ref[hh] + jnp.dot(vt, p.astype(BF16), preferred_element_type=F32)
            m_ref[hh] = m_new

    def body(j, carry):
        block(j, False)
        return carry

    lax.fori_loop(0, qi, body, 0)
    block(qi, True)
    o_t = jnp.concatenate([acc_ref[0, 0:HEAD_DIM, :] / l_ref[0], acc_ref[1, HEAD_DIM:LANES, :] / l_ref[1]], axis=0)
    o_ref[0] = o_t.T.astype(o_ref.dtype)


def _attn_prompt(qa, ka, vt, tq):
    b, n_heads, s, _ = qa.shape
    q_spec = lambda hh: pl.BlockSpec((1, 1, tq, LANES), lambda bi, p, qi: (bi, 2 * p + hh, qi, 0))
    k_spec = lambda hh: pl.BlockSpec((1, 1, s, LANES), lambda bi, p, qi: (bi, 2 * p + hh, 0, 0))
    return pl.pallas_call(
        functools.partial(_attn_prompt_kernel, tq=tq),
        grid=(b, n_heads // 2, s // tq),
        in_specs=[q_spec(0), q_spec(1), k_spec(0), k_spec(1),
                  pl.BlockSpec((1, 1, LANES, s), lambda bi, p, qi: (bi, p, 0, 0))],
        out_specs=pl.BlockSpec((1, tq, LANES), lambda bi, p, qi: (bi, qi, p)),
        out_shape=jax.ShapeDtypeStruct((b, s, n_heads * HEAD_DIM), BF16),
        scratch_shapes=[pltpu.VMEM((2, 1, tq), F32), pltpu.VMEM((2, 1, tq), F32),
                        pltpu.VMEM((2, LANES, tq), F32)],
        compiler_params=_params(("arbitrary",) * 3),
        name="attn_prompt",
    )(qa, qa, ka, ka, vt)


def _page_sums_kernel(lf_ref, w_ref, sfx_ref, tot_ref):
    r = _dot3(lf_ref[...], w_ref[...])
    sfx_ref[...] = r[:, 0:PAGE]
    tot_ref[...] = r[:, PAGE:2 * PAGE]


def _page_sums(lf_rows, rows):
    n = lf_rows.shape[0]
    pos = np.arange(PAGE)
    w_b = jnp.asarray(np.concatenate([pos[:, None] > pos[None, :], np.ones((PAGE, PAGE), bool)], axis=1), BF16)
    blk = pl.BlockSpec((rows, PAGE), lambda i: (i, 0))
    return pl.pallas_call(
        _page_sums_kernel,
        grid=(n // rows,),
        in_specs=[blk, _full(w_b.shape)],
        out_specs=(blk, blk),
        out_shape=(jax.ShapeDtypeStruct((n, PAGE), F32), jax.ShapeDtypeStruct((n, PAGE), F32)),
        compiler_params=_params(("arbitrary",)),
        name="page_sums",
    )(lf_rows, w_b)


def _attn_sample_kernel(pt_ref, q_ref, kn_ref, vn_ref, lfn_ref, *rest, pps, n_heads, t_new):
    k_refs = rest[0:pps]
    v_refs = rest[pps:2 * pps]
    sfx_refs = rest[2 * pps:3 * pps]
    tot_refs = rest[3 * pps:4 * pps]
    o_ref = rest[4 * pps]
    qbd_ref, cq_ref, m_ref, l_ref, acc_ref, carry_ref = rest[4 * pps + 1:]
    j = pl.program_id(1)
    nj = pl.num_programs(1)
    d = n_heads * HEAD_DIM
    rows = t_new * n_heads
    contract_last = (((1,), (1,)), ((), ()))
    head_of_lane = lax.broadcasted_iota(jnp.int32, (n_heads, d), 1) // HEAD_DIM
    head_mask = head_of_lane == lax.broadcasted_iota(jnp.int32, (n_heads, d), 0)
    per_token = lambda tile: jnp.concatenate([tile] * t_new, axis=0)

    @pl.when(j == 0)
    def _():
        q = q_ref[0].astype(F32)
        cn = jnp.dot(lfn_ref[0], _upper_tri(PAGE), precision=HIGHEST, preferred_element_type=F32)
        for t in range(t_new):
            qt = jnp.broadcast_to(q[t:t + 1, :], (n_heads, d))
            qbd_ref[t * n_heads:(t + 1) * n_heads, :] = jnp.where(head_mask, qt, jnp.zeros_like(qt))
            cq_ref[t * n_heads:(t + 1) * n_heads, :] = jnp.broadcast_to(cn[:, t:t + 1], (n_heads, PAGE))
        pad = jnp.zeros((PAGE - t_new, d), F32)
        kb = jnp.concatenate([kn_ref[0], pad], axis=0).astype(BF16)
        vb = jnp.concatenate([vn_ref[0], pad], axis=0).astype(BF16)
        s = lax.dot_general(qbd_ref[...].astype(BF16), kb, contract_last, preferred_element_type=F32)
        s = s + (cq_ref[...] - per_token(cn))
        key = lax.broadcasted_iota(jnp.int32, (rows, PAGE), 1)
        tok = lax.broadcasted_iota(jnp.int32, (rows, PAGE), 0) // n_heads
        s = jnp.where(key <= tok, s, -jnp.inf)
        m = jnp.max(s, axis=1, keepdims=True)
        p = jnp.exp(s - m)
        m_ref[...] = m
        l_ref[...] = jnp.sum(p, axis=1, keepdims=True)
        acc_ref[...] = jnp.dot(p.astype(BF16), vb, preferred_element_type=F32)
        carry_ref[...] = jnp.zeros_like(carry_ref)

    qbd = qbd_ref[...].astype(BF16)
    cq = cq_ref[...]
    carry = carry_ref[...]
    m_prev = m_ref[...]
    m_new = m_prev
    scores = []
    for i in range(pps):
        kt = k_refs[i][0].reshape(d, PAGE).astype(BF16)
        s = jnp.dot(qbd, kt, preferred_element_type=F32)
        s = s + (cq + per_token(carry + sfx_refs[i][0]))
        scores.append(s)
        m_new = jnp.maximum(m_new, jnp.max(s, axis=1, keepdims=True))
        carry = carry + tot_refs[i][0]
    carry_ref[...] = carry
    alpha = jnp.exp(m_prev - m_new)
    l = alpha * l_ref[...]
    acc = alpha * acc_ref[...]
    for i in range(pps):
        p = jnp.exp(scores[i] - m_new)
        l = l + jnp.sum(p, axis=1, keepdims=True)
        vt = v_refs[i][0].reshape(d, PAGE).astype(BF16)
        acc = acc + lax.dot_general(p.astype(BF16), vt, contract_last, preferred_element_type=F32)
    m_ref[...] = m_new
    l_ref[...] = l
    acc_ref[...] = acc

    @pl.when(j == nj - 1)
    def _():
        o = acc / l
        out_rows = []
        for t in range(t_new):
            ot = jnp.where(head_mask, o[t * n_heads:(t + 1) * n_heads, :], 0.0)
            out_rows.append(jnp.sum(ot, axis=0, keepdims=True))
        o_ref[0] = jnp.concatenate(out_rows, axis=0)


def _attn_sample(page_table, qb, kt_cache, vt_cache, k_new, v_new, lf_new_t, sfx, tot, pps):
    db, t_new, d = qb.shape
    n_heads = kt_cache.shape[1]
    n_pages = page_table.shape[1]
    assert n_pages % pps == 0
    rows = t_new * n_heads
    per_seq = lambda b, j, pt: (b, 0, 0)

    def page4(i):
        return lambda b, j, pt: (pt[b, n_pages - 1 - (j * pps + i)], 0, 0, 0)

    def page3(i):
        return lambda b, j, pt: (pt[b, n_pages - 1 - (j * pps + i)], 0, 0)

    in_specs = [pl.BlockSpec((1, t_new, d), per_seq), pl.BlockSpec((1, t_new, d), per_seq),
                pl.BlockSpec((1, t_new, d), per_seq), pl.BlockSpec((1, n_heads, PAGE), per_seq)]
    in_specs += [pl.BlockSpec((1, n_heads, HEAD_DIM, PAGE), page4(i)) for i in range(pps)]
    in_specs += [pl.BlockSpec((1, n_heads, HEAD_DIM, PAGE), page4(i)) for i in range(pps)]
    in_specs += [pl.BlockSpec((1, n_heads, PAGE), page3(i)) for i in range(pps)]
    in_specs += [pl.BlockSpec((1, n_heads, PAGE), page3(i)) for i in range(pps)]
    grid_spec = pltpu.PrefetchScalarGridSpec(
        num_scalar_prefetch=1,
        grid=(db, n_pages // pps),
        in_specs=in_specs,
        out_specs=pl.BlockSpec((1, t_new, d), per_seq),
        scratch_shapes=[pltpu.VMEM((rows, d), F32), pltpu.VMEM((rows, PAGE), F32),
                        pltpu.VMEM((rows, 1), F32), pltpu.VMEM((rows, 1), F32),
                        pltpu.VMEM((rows, d), F32), pltpu.VMEM((n_heads, PAGE), F32)],
    )
    args = [page_table, qb, k_new, v_new, lf_new_t] + [kt_cache] * pps + [vt_cache] * pps + [sfx] * pps + [tot] * pps
    return pl.pallas_call(
        functools.partial(_attn_sample_kernel, pps=pps, n_heads=n_heads, t_new=t_new),
        grid_spec=grid_spec,
        out_shape=jax.ShapeDtypeStruct((db, t_new, d), F32),
        compiler_params=_params(("arbitrary", "arbitrary")),
        name="attn_sample",
    )(*args)


def _merge_kernel(x_ref, pool_ref, attn_ref, gmix_ref, wg_ref, wbp_ref, wba_ref, wo_ref, gffn_ref, wq_ref,
                  h_ref, hnt_ref, qp_ref):
    x = x_ref[...]
    dm = x.shape[1]
    ms = jnp.mean(x * x, axis=-1, keepdims=True)
    xn = (x * lax.rsqrt(ms + RMS_EPS) * gmix_ref[...]).astype(BF16)
    gates = jax.nn.sigmoid(jnp.dot(xn, wg_ref[...], preferred_element_type=F32))
    mixed = (gates[:, 0:dm] * jnp.dot(pool_ref[...], wbp_ref[...], preferred_element_type=F32)
             + gates[:, dm:2 * dm] * jnp.dot(attn_ref[...], wba_ref[...], preferred_element_type=F32))
    h = x + jnp.dot(mixed.astype(BF16), wo_ref[...], preferred_element_type=F32)
    h_ref[...] = h
    hms = jnp.mean(h * h, axis=-1, keepdims=True)
    hn = h * lax.rsqrt(hms + RMS_EPS) * gffn_ref[...]
    hnb = hn.astype(BF16)
    hnt_ref[...] = hn.T.astype(BF16)
    qp_ref[...] = jnp.dot(hnb, wq_ref[...], preferred_element_type=F32).astype(BF16)


def _merge(x, pool, attn, g_mix, w_gates, wbp, wba, wo, g_ffn, wq, tm):
    n, dm = x.shape
    dq = wq.shape[1]
    row = lambda w: pl.BlockSpec((tm, w), lambda i: (i, 0))
    return pl.pallas_call(
        _merge_kernel,
        grid=(n // tm,),
        in_specs=[row(dm), row(pool.shape[1]), row(attn.shape[1]), _full(g_mix.shape), _full(w_gates.shape),
                  _full(wbp.shape), _full(wba.shape), _full(wo.shape), _full(g_ffn.shape), _full(wq.shape)],
        out_specs=(row(dm), pl.BlockSpec((dm, tm), lambda i: (0, i)), row(dq)),
        out_shape=(jax.ShapeDtypeStruct((n, dm), F32),
                   jax.ShapeDtypeStruct((dm, n), BF16),
                   jax.ShapeDtypeStruct((n, dq), BF16)),
        compiler_params=_params(("arbitrary",)),
        name="merge",
    )(x, pool, attn, g_mix, w_gates, wbp, wba, wo, g_ffn, wq)


def _oddeven_merge_sort_pairs(n):
    pairs = []
    p = 1
    while p < n:
        k = p
        while k >= 1:
            for j in range(k % p, n - k, 2 * k):
                for i in range(min(k, n - j - k)):
                    if (i + j) // (2 * p) == (i + j + k) // (2 * p):
                        pairs.append((i + j, i + j + k))
            k //= 2
        p *= 2
    return pairs


def _bitonic_merge_pairs(n):
    pairs = []
    k = n // 2
    while k >= 1:
        for i in range(n):
            if (i // k) % 2 == 0 and i + k < n:
                pairs.append((i, i + k))
        k //= 2
    return pairs


_SORT16 = _oddeven_merge_sort_pairs(16)
_MERGE16 = _bitonic_merge_pairs(16)


def _apply_desc(vals, pairs):
    vals = list(vals)
    for a, b in pairs:
        hi = jnp.maximum(vals[a], vals[b])
        lo = jnp.minimum(vals[a], vals[b])
        vals[a], vals[b] = hi, lo
    return vals


def _top16_across_sublanes(vals):
    for shift in (4, 2, 1):
        other = [pltpu.roll(v, shift, axis=0) for v in vals]
        vals = [jnp.maximum(vals[i], other[PEER_TOPK - 1 - i]) for i in range(PEER_TOPK)]
        vals = _apply_desc(vals, _MERGE16)
    return vals


def _max_below(vals, limit):
    m = jnp.where(vals[0] < limit, vals[0], -jnp.inf)
    for v in vals[1:]:
        m = jnp.maximum(m, jnp.where(v < limit, v, -jnp.inf))
    for shift in (4, 2, 1):
        m = jnp.maximum(m, pltpu.roll(m, shift, axis=0))
    return m


def _topk_tile(s_tile):
    vals = [s_tile[SUBLANES * i:SUBLANES * (i + 1), :] for i in range(N_KEYS // SUBLANES)]
    top = _top16_across_sublanes(_apply_desc(vals, _SORT16))
    return top, _max_below(vals, top[PEER_TOPK - 1])


def _route_kernel(qp_ref, sk_ref, s1_ref, s2_ref, st_ref, *, n_heads, lt):
    tn = qp_ref.shape[0]
    neg = -jnp.inf
    sub = lax.broadcasted_iota(jnp.int32, (SUBLANES, lt), 0)
    for h in range(n_heads):
        for half, dst in ((0, s1_ref), (1, s2_ref)):
            c0 = (2 * h + half) * LANES
            dst[h] = lax.dot_general(sk_ref[2 * h + half], qp_ref[:, c0:c0 + LANES],
                                     (((1,), (1,)), ((), ())), preferred_element_type=F32)

    def body(it, carry):
        h = it // (tn // lt)
        l0 = pl.multiple_of((it % (tn // lt)) * lt, lt)
        a, a_next = _topk_tile(s1_ref[h, :, pl.ds(l0, lt)])
        b, b_next = _topk_tile(s2_ref[h, :, pl.ds(l0, lt)])
        def pack(vs):
            out = vs[0]
            for i in range(1, len(vs)):
                out = jnp.where(sub == i, vs[i], out)
            return out
        b_lo, b_hi, a_hi = pack(b[0:8]), pack(b[8:16]), pack(a[8:16])
        cands = [a[0] + b_lo, a[0] + b_hi, a[1] + b_lo]
        for k in range(2, 8):
            cands.append(jnp.where(sub < PEER_TOPK // (k + 1), a[k] + b_lo, neg))
        cands.append(a_hi + b[0])
        padded = cands + [jnp.full((SUBLANES, lt), neg, F32)] * (PEER_TOPK - len(cands))
        top = _top16_across_sublanes(_apply_desc(padded, _SORT16))
        z = jnp.zeros((SUBLANES, lt), F32)
        for i in range(PEER_TOPK):
            z = z + jnp.exp(top[i] - top[0])
        t17 = jnp.maximum(_max_below(cands, top[PEER_TOPK - 1]), jnp.maximum(a[0] + b_next, a_next + b[0]))
        tau = 0.5 * (top[PEER_TOPK - 1] + t17)
        row = lambda v: v[0:1, :]
        st_ref[0, h, :, pl.ds(l0, lt)] = row(tau)
        st_ref[1, h, :, pl.ds(l0, lt)] = row(a[0])
        st_ref[2, h, :, pl.ds(l0, lt)] = row(b[0])
        st_ref[3, h, :, pl.ds(l0, lt)] = row(1.0 / z)
        return carry

    lax.fori_loop(0, n_heads * (tn // lt), body, 0)


def _route(qp, sub_keys_b, n_heads, tn, lt=LANES):
    n, dq = qp.shape
    return pl.pallas_call(
        functools.partial(_route_kernel, n_heads=n_heads, lt=lt),
        grid=(n // tn,),
        in_specs=[pl.BlockSpec((tn, dq), lambda i: (i, 0)), _full(sub_keys_b.shape)],
        out_specs=(pl.BlockSpec((n_heads, N_KEYS, tn), lambda i: (0, 0, i)),
                   pl.BlockSpec((n_heads, N_KEYS, tn), lambda i: (0, 0, i)),
                   pl.BlockSpec((4, n_heads, 1, tn), lambda i: (0, 0, 0, i))),
        out_shape=(jax.ShapeDtypeStruct((n_heads, N_KEYS, n), F32),
                   jax.ShapeDtypeStruct((n_heads, N_KEYS, n), F32),
                   jax.ShapeDtypeStruct((4, n_heads, 1, n), F32)),
        compiler_params=_params(("arbitrary",)),
        name="peer_route",
    )(qp, sub_keys_b)


def _tree_sum(terms):
    while len(terms) > 1:
        terms = [terms[i] + terms[i + 1] for i in range(0, len(terms) - 1, 2)] + \
                ([terms[-1]] if len(terms) % 2 else [])
    return terms[0]


def _peer_kernel(hnt_ref, s1_ref, s2_ref, st_ref, ua_ref, ub_ref, vta_ref, vtb_ref, h_ref, y_ref,
                 p1_ref, se_ref, sp_ref, a_ref, wg_ref, acc_ref, *, n_heads, rows_per_step):
    s = pl.program_id(1)
    n_blocks = pl.num_programs(1) - 1
    tt = hnt_ref.shape[1]
    n_lt = tt // LANES
    n_jv = N_KEYS // SUBLANES
    half = tt // 2
    first, second = slice(0, half), slice(half, tt)
    lane_tile = lambda tl: slice(tl * LANES, (tl + 1) * LANES)

    def activations(u_rows, cols):
        a_ref[:, cols] = jnp.dot(u_rows[...], hnt_ref[:, cols], preferred_element_type=F32)

    def accumulate(vt_rows, cols):
        acc_ref[:, cols] += jnp.dot(vt_rows[...], wg_ref[:, cols].astype(BF16), preferred_element_type=F32)

    @pl.when(s == 0)
    def _():
        acc_ref[...] = jnp.zeros_like(acc_ref)
        wg_ref[:, second] = jnp.zeros((wg_ref.shape[0], half), F32)
        for h in range(n_heads):
            p1_ref[h] = jnp.exp(s1_ref[h] - st_ref[1, h]) * (0.5 * st_ref[3, h])
            e2h = jnp.exp(s2_ref[h] - st_ref[2, h])
            for tl in range(n_lt):
                se_ref[tl, :, h] = e2h[:, lane_tile(tl)].reshape(n_jv, SUBLANES, LANES)
        activations(ub_ref, first)

    def key_tile(tl, jv):
        e2 = [se_ref[tl, jv, h] for h in range(n_heads)]
        for r in range(rows_per_step):
            terms = [jnp.where(e2[h] >= sp_ref[tl, h, r, 0], e2[h] * sp_ref[tl, h, r, 1], 0.0)
                     for h in range(n_heads)]
            rows = slice(r * N_KEYS + jv * SUBLANES, r * N_KEYS + (jv + 1) * SUBLANES)
            act = a_ref[rows, lane_tile(tl)]
            gelu = act * (1.0 + lax.erf(act * np.float32(1.0 / np.sqrt(2.0))))
            wg_ref[rows, lane_tile(tl)] = _tree_sum(terms) * gelu

    def routing_weights(lane_tiles):
        for tl in lane_tiles:
            for jv in range(n_jv):
                key_tile(tl, jv)

    @pl.when(s >= 1)
    def _():
        i0 = pl.multiple_of((s - 1) * rows_per_step, SUBLANES)
        for h in range(n_heads):
            s1_rows = s1_ref[h, pl.ds(i0, rows_per_step), :]
            p1_rows = p1_ref[h, pl.ds(i0, rows_per_step), :]
            bound = jnp.exp((st_ref[0, h] - s1_rows) - st_ref[2, h])
            for r in range(rows_per_step):
                for tl in range(n_lt):
                    sp_ref[tl, h, r, 0] = jnp.broadcast_to(bound[r:r + 1, lane_tile(tl)], (SUBLANES, LANES))
                    sp_ref[tl, h, r, 1] = jnp.broadcast_to(p1_rows[r:r + 1, lane_tile(tl)], (SUBLANES, LANES))
        activations(ua_ref, second)
        accumulate(vta_ref, second)
        routing_weights(range(0, n_lt // 2))
        accumulate(vtb_ref, first)
        activations(ub_ref, first)
        routing_weights(range(n_lt // 2, n_lt))

    @pl.when(s == n_blocks)
    def _():
        accumulate(vtb_ref, second)
        y_ref[...] = h_ref[...] + acc_ref[...].T


def _peer(hnt, s1, s2, stats, u_b, vt_b, h, tt, rows_per_step):
    dm, n = hnt.shape
    n_heads = s1.shape[0]
    n_exp = u_b.shape[0]
    eb = rows_per_step * N_KEYS
    nb = n_exp // eb
    blk = lambda off: (lambda t, s: jnp.clip(s + off, 0, nb - 1))
    u_spec = lambda off: pl.BlockSpec((eb, dm), lambda t, s: (blk(off)(t, s), 0))
    vt_spec = lambda off: pl.BlockSpec((dm, eb), lambda t, s: (0, blk(off)(t, s)))
    return pl.pallas_call(
        functools.partial(_peer_kernel, n_heads=n_heads, rows_per_step=rows_per_step),
        grid=(n // tt, nb + 1),
        in_specs=[
            pl.BlockSpec((dm, tt), lambda t, s: (0, t)),
            pl.BlockSpec((n_heads, N_KEYS, tt), lambda t, s: (0, 0, t)),
            pl.BlockSpec((n_heads, N_KEYS, tt), lambda t, s: (0, 0, t)),
            pl.BlockSpec((4, n_heads, 1, tt), lambda t, s: (0, 0, 0, t)),
            u_spec(-1), u_spec(0), vt_spec(-2), vt_spec(-1),
            pl.BlockSpec((tt, dm), lambda t, s: (t, 0)),
        ],
        out_specs=pl.BlockSpec((tt, dm), lambda t, s: (t, 0)),
        out_shape=jax.ShapeDtypeStruct((n, dm), F32),
        scratch_shapes=[pltpu.VMEM((n_heads, N_KEYS, tt), F32),
                        pltpu.VMEM((tt // LANES, N_KEYS // SUBLANES, n_heads, SUBLANES, LANES), F32),
                        pltpu.VMEM((tt // LANES, n_heads, rows_per_step, 2, SUBLANES, LANES), F32),
                        pltpu.VMEM((eb, tt), F32), pltpu.VMEM((eb, tt), F32), pltpu.VMEM((dm, tt), F32)],
        compiler_params=_params(("arbitrary", "arbitrary")),
        name="peer_dense",
    )(hnt, s1, s2, stats, u_b, u_b, vt_b, vt_b, h)


def _layer(x_p, x_s, cache_k, cache_v, cache_logf, state_pool, page_table, norm_mix_g, w_in, b_forget,
           q_norm_g, k_norm_g, pool_mix_w, pool_scale, w_branch_pool, w_branch_attn, w_out, norm_ffn_g,
           w_query, sub_keys, peer_u, peer_v):
    b, s, dm = x_p.shape
    db, t_new, _ = x_s.shape
    n_heads = b_forget.shape[0]
    d = n_heads * HEAD_DIM
    n_p, n_s = b * s, db * t_new
    past_len = page_table.shape[1] * PAGE
    tm = ROW_TILE

    w_a = jnp.concatenate([w_in[:, 0:4 * d + n_heads],
                           jnp.zeros((dm, LANES - n_heads), w_in.dtype)], axis=1).astype(BF16)
    w_gates = w_in[:, 4 * d + n_heads:].astype(BF16)
    b_f = jnp.concatenate([b_forget, jnp.zeros((LANES - n_heads,), F32)]).reshape(1, LANES)
    qg = jnp.tile(q_norm_g, n_heads).reshape(1, d)
    kg = jnp.tile(k_norm_g, n_heads).reshape(1, d)
    lane_head = np.arange(d) // HEAD_DIM
    gmat = jnp.asarray(lane_head[:, None] == lane_head[None, :], BF16)
    g_mix = norm_mix_g.reshape(1, dm)
    g_ffn = norm_ffn_g.reshape(1, dm)
    w_mix = pool_mix_w.astype(BF16)
    p_scale = pool_scale.reshape(1, d)

    wbp, wba, wo, wq = (w.astype(BF16) for w in (w_branch_pool, w_branch_attn, w_out, w_query))
    n_ph = sub_keys.shape[0]
    sk_b = sub_keys.reshape(n_ph * 2, N_KEYS, sub_keys.shape[-1]).astype(BF16)
    u_b, vt_b = peer_u.astype(BF16), jnp.transpose(peer_v).astype(BF16)

    def merge_and_ffn(x2, pool, attn):
        h, hnt, qp = _merge(x2, pool, attn, g_mix, w_gates, wbp, wba, wo, g_ffn, wq, tm)
        s1, s2, stats = _route(qp, sk_b, n_ph, ROUTE_TILE)
        return _peer(hnt, s1, s2, stats, u_b, vt_b, h, PEER_TOKENS, PEER_ROWS)

    x2_p = x_p.reshape(n_p, dm)
    u, qb, kt, kb, vt, vtb, lf = _project(x2_p, g_mix, w_a, b_f, qg, kg, gmat, n_heads, tm, seq_len=s)
    pool_p, hist16 = _pool_prompt(u.reshape(b, s, d), w_mix, p_scale, tm)
    hist_p = hist16[:, 1:, :]
    lf_p = lf.reshape(b, s, n_heads)
    c_t = _cumsum_prompt(jnp.transpose(lf_p, (0, 2, 1)), CUMSUM_TILE)
    qa, ka = _augment(qb.reshape(b, s, d), kb.reshape(b, s, d), jnp.transpose(c_t, (0, 2, 1)), tm)
    attn_p = _attn_prompt(qa, ka, vtb.reshape(b, n_heads // 2, LANES, s), ATTN_TILE)
    y_p = merge_and_ffn(x2_p, pool_p.reshape(n_p, d), attn_p.reshape(n_p, d)).reshape(b, s, dm)
    k_p = jnp.transpose(kt.reshape(b, n_heads, HEAD_DIM, s), (0, 3, 1, 2))
    v_p = jnp.transpose(vt.reshape(b, n_heads, HEAD_DIM, s), (0, 3, 1, 2))

    x2_s = x_s.reshape(n_s, dm)
    u, qb, k, _, v, _, lf = _project(x2_s, g_mix, w_a, b_f, qg, kg, gmat, n_heads, tm)
    u_s = u.reshape(db, t_new, d)
    state16 = jnp.concatenate([jnp.zeros((db, 1, d), F32), state_pool], axis=1)
    pool_s = _pool_sample(u_s, state16, w_mix, p_scale, past_len)
    hist_s = jnp.concatenate([state_pool, u_s], axis=1)[:, -POOL_HIST:, :]
    lf_s = lf.reshape(db, t_new, n_heads)
    n_phys = cache_k.shape[0]
    lf_rows = jnp.transpose(cache_logf, (0, 2, 1)).reshape(n_phys * n_heads, PAGE)
    sfx, tot = _page_sums(lf_rows, PAGE_SUM_ROWS)
    lf_new_t = jnp.concatenate([jnp.transpose(lf_s, (0, 2, 1)),
                                jnp.zeros((db, n_heads, PAGE - t_new), F32)], axis=2)
    attn_s = _attn_sample(page_table, qb.reshape(db, t_new, d),
                          jnp.transpose(cache_k, (0, 2, 3, 1)), jnp.transpose(cache_v, (0, 2, 3, 1)),
                          k.reshape(db, t_new, d), v.reshape(db, t_new, d), lf_new_t,
                          sfx.reshape(n_phys, n_heads, PAGE), tot.reshape(n_phys, n_heads, PAGE),
                          PAGES_PER_STEP)
    y_s = merge_and_ffn(x2_s, pool_s, attn_s.reshape(n_s, d).astype(BF16)).reshape(db, t_new, dm)
    k_s = k.reshape(db, t_new, n_heads, HEAD_DIM)
    v_s = v.reshape(db, t_new, n_heads, HEAD_DIM)
    return y_p, y_s, k_p, v_p, lf_p, hist_p, k_s, v_s, lf_s, hist_s


def kernel(x_prompt, x_sample, cache_k, cache_v, cache_logf, state_pool, page_table, norm_mix_g, w_in,
           b_forget, q_norm_g, k_norm_g, pool_mix_w, pool_scale, w_branch_pool, w_branch_attn, w_out,
           norm_ffn_g, w_query, sub_keys, peer_u, peer_v):
    depth = w_in.shape[0]
    x_p, x_s = x_prompt, x_sample
    outs = [[] for _ in range(8)]
    for l in range(depth):
        res = _layer(x_p, x_s, cache_k[l], cache_v[l], cache_logf[l], state_pool[l], page_table,
                     norm_mix_g[l], w_in[l], b_forget[l], q_norm_g[l], k_norm_g[l], pool_mix_w[l],
                     pool_scale[l], w_branch_pool[l], w_branch_attn[l], w_out[l], norm_ffn_g[l],
                     w_query[l], sub_keys[l], peer_u[l], peer_v[l])
        x_p, x_s = res[0], res[1]
        for acc, r in zip(outs, res[2:]):
            acc.append(r)
    return (x_p, x_s) + tuple(jnp.stack(o) for o in outs)
```
